```python
import jax, jax.numpy as jnp
from jax import lax
import numpy as np

D_MODEL = 2048
BATCH = 2
SEQ = 8192
DEPTH = 4

GRID_W = 64
CTX_LEN = 256
N_MIXERS = 3
RMS_EPS = 1e-6
NEG_INF = -1e30

ATTN_HEADS = 16
ATTN_KV_HEADS = 4
ATTN_GROUP = ATTN_HEADS // ATTN_KV_HEADS
HEAD_DIM = D_MODEL // ATTN_HEADS
WINDOW = 128
ATTN_BLOCK = 128
ROPE_THETA = 10000.0

FNET_GROUPS = 8
FNET_GROUP_DIM = D_MODEL // FNET_GROUPS

HGRN_HEADS = 16
HGRN_DK = 128
HGRN_DV = D_MODEL // HGRN_HEADS
HGRN_CHUNK = 64

FFN_DIM = 5632
N_EXPERTS = 8
TOP_K = 2
EXPERT_DIM = 7168

kernel_name = 'hybrid_dit_gqa_fnet_hgrn2_moe'


def rms_norm(t, gain):
    tf = t.astype(jnp.float32)
    y = tf * lax.rsqrt(jnp.mean(tf * tf, axis=-1, keepdims=True) + RMS_EPS)
    return (y * gain.astype(jnp.float32)).astype(t.dtype)


def ada_mod(cond, w, b):
    m = jax.nn.silu(cond) @ w + b
    return jnp.split(m[..., None, :], 6, axis=-1)


def modulate(t, gain, shift, scale):
    return rms_norm(t, gain) * (1 + scale) + shift


def axial_rope_angles(n_tokens):
    rows = n_tokens // GRID_W
    row = jnp.repeat(jnp.arange(rows, dtype=jnp.float32), GRID_W)
    col = jnp.tile(jnp.arange(GRID_W, dtype=jnp.float32), rows)
    sec = HEAD_DIM // 2
    inv = ROPE_THETA ** (-jnp.arange(0, sec, 2, dtype=jnp.float32) / sec)
    return row[:, None] * inv, col[:, None] * inv


def rope_section(t, ang):
    half = t.shape[-1] // 2
    t1, t2 = t[..., :half], t[..., half:]
    cos, sin = jnp.cos(ang), jnp.sin(ang)
    return jnp.concatenate([t1 * cos - t2 * sin, t2 * cos + t1 * sin], axis=-1)


def apply_axial_rope(t, ang_row, ang_col):
    shape = (t.shape[1],) + (1,) * (t.ndim - 3) + (-1,)
    sec = HEAD_DIM // 2
    out = jnp.concatenate([rope_section(t[..., :sec], ang_row.reshape(shape)),
                           rope_section(t[..., sec:], ang_col.reshape(shape))], axis=-1)
    return out.astype(t.dtype)


def gqa_project(u, w_qkv):
    b, n, _ = u.shape
    p = u @ w_qkv
    qd, kd = ATTN_HEADS * HEAD_DIM, ATTN_KV_HEADS * HEAD_DIM
    q = p[..., :qd].reshape(b, n, ATTN_KV_HEADS, ATTN_GROUP, HEAD_DIM)
    k = p[..., qd:qd + kd].reshape(b, n, ATTN_KV_HEADS, HEAD_DIM)
    v = p[..., qd + kd:].reshape(b, n, ATTN_KV_HEADS, HEAD_DIM)
    return q, k, v


def band_windows(t, nb):
    b = t.shape[0]
    tp = jnp.pad(t, ((0, 0), (ATTN_BLOCK, ATTN_BLOCK), (0, 0), (0, 0)))
    tp = tp.reshape(b, nb + 2, ATTN_BLOCK, *t.shape[2:])
    return jnp.concatenate([tp[:, :-2], tp[:, 1:-1], tp[:, 2:]], axis=2)


def windowed_gqa_mixer(uc, ux, w_qkv, w_o, sink, ang_row, ang_col, with_ctx_out):
    B, S, _ = ux.shape
    L = uc.shape[1]
    nb = S // ATTN_BLOCK
    scale = HEAD_DIM ** -0.5
    sink_g = sink.astype(jnp.float32).reshape(ATTN_KV_HEADS, ATTN_GROUP, 1, 1)
    qc, kc, vc = gqa_project(uc, w_qkv)
    qx, kx, vx = gqa_project(ux, w_qkv)
    qx = apply_axial_rope(qx, ang_row, ang_col)
    kx = apply_axial_rope(kx, ang_row, ang_col)

    oc = None
    if with_ctx_out:
        s_c = jnp.einsum('blhgd,bkhd->bhglk', qc, kc, preferred_element_type=jnp.float32) * scale
        s_c = jnp.concatenate([jnp.broadcast_to(sink_g, s_c.shape[:-1] + (1,)), s_c], axis=-1)
        p_c = jax.nn.softmax(s_c, axis=-1)[..., 1:].astype(vc.dtype)
        oc = jnp.einsum('bhglk,bkhd->blhgd', p_c, vc).reshape(B, L, -1) @ w_o

    qb = qx.reshape(B, nb, ATTN_BLOCK, ATTN_KV_HEADS, ATTN_GROUP, HEAD_DIM)
    kw, vw = band_windows(kx, nb), band_windows(vx, nb)
    s_ctx = jnp.einsum('bnqhgd,bkhd->bnhgqk', qb, kc, preferred_element_type=jnp.float32) * scale
    s_band = jnp.einsum('bnqhgd,bnkhd->bnhgqk', qb, kw, preferred_element_type=jnp.float32) * scale
    qpos = jnp.arange(S).reshape(nb, ATTN_BLOCK)
    kpos = (jnp.arange(nb)[:, None] - 1) * ATTN_BLOCK + jnp.arange(3 * ATTN_BLOCK)[None, :]
    valid = ((jnp.abs(kpos[:, None, :] - qpos[:, :, None]) <= WINDOW)
             & (kpos >= 0)[:, None, :] & (kpos < S)[:, None, :])
    s_band = jnp.where(valid[None, :, None, None], s_band, NEG_INF)
    logits = jnp.concatenate([jnp.broadcast_to(sink_g, s_ctx.shape[:-1] + (1,)), s_ctx, s_band], axis=-1)
    p = jax.nn.softmax(logits, axis=-1).astype(vx.dtype)
    ox = (jnp.einsum('bnhgqk,bkhd->bnqhgd', p[..., 1:1 + L], vc)
          + jnp.einsum('bnhgqk,bnkhd->bnqhgd', p[..., 1 + L:], vw))
    return oc, ox.reshape(B, S, -1) @ w_o


def fourier_tokens(u):
    b, n, d = u.shape
    g = u.astype(jnp.float32).reshape(b, n, FNET_GROUPS, FNET_GROUP_DIM)
    y = jnp.fft.fft2(g, axes=(1, 3), norm='ortho').real
    return y.reshape(b, n, d).astype(u.dtype)


def fourier_mixer(uc, ux, w_o, with_ctx_out):
    oc = fourier_tokens(uc) @ w_o if with_ctx_out else None
    return oc, fourier_tokens(ux) @ w_o


def hgrn_lower_bounds(lb_logits, layer):
    gamma = jax.nn.softmax(lb_logits.astype(jnp.float32), axis=0)
    lb = jnp.cumsum(gamma, axis=0) - gamma[0]
    return lb[layer]


def hgrn_project(u, w_in, lb):
    b, n, d = u.shape
    hk = HGRN_HEADS * HGRN_DK
    p = (u @ w_in).astype(jnp.float32)
    q = jax.nn.silu(p[..., :hk]).reshape(b, n, HGRN_HEADS, HGRN_DK)
    i = p[..., 3 * hk:3 * hk + d].reshape(b, n, HGRN_HEADS, HGRN_DV)
    g = p[..., 3 * hk + d:]
    dirs = []
    for dr in range(2):
        f = lb[dr] + (1 - lb[dr]) * jax.nn.sigmoid(p[..., (1 + dr) * hk:(2 + dr) * hk])
        f = f.reshape(b, n, HGRN_HEADS, HGRN_DK)
        dirs.append((1 - f, jnp.log(f)))
    return q, i, g, dirs


def gated_chunk_scan(q, k, v, log_f, s0):
    B, T, H, K = q.shape
    n = T // HGRN_CHUNK

    def blk(a):
        return a.reshape(B, n, HGRN_CHUNK, H, a.shape[-1]).transpose(1, 0, 3, 2, 4)

    q, k, v, log_f = blk(q), blk(k), blk(v), blk(log_f)
    b = jnp.cumsum(log_f, axis=3)
    b_last = b[:, :, :, -1:, :]
    q_in = q * jnp.exp(b)
    k_in = k * jnp.exp(-b)
    k_out = k * jnp.exp(b_last - b)
    mask = jnp.tril(jnp.ones((HGRN_CHUNK, HGRN_CHUNK), dtype=bool))
    a = jnp.where(mask, jnp.einsum('nbhtk,nbhsk->nbhts', q_in, k_in), 0.0)
    o_intra = jnp.einsum('nbhts,nbhsv->nbhtv', a, v)
    decay = jnp.exp(b_last[:, :, :, 0, :])

    def step(s, xs):
        q_c, k_c, v_c, d_c = xs
        o = jnp.einsum('bhtk,bhkv->bhtv', q_c, s)
        s = d_c[..., None] * s + jnp.einsum('bhtk,bhtv->bhkv', k_c, v_c)
        return s, o

    s_final, o_inter = lax.scan(step, s0, (q_in, k_out, v, decay))
    o = (o_intra + o_inter).transpose(1, 0, 3, 2, 4).reshape(B, T, H, v.shape[-1])
    return o, s_final


def directional_scan(q, k, v, log_f, s0, reverse):
    if not reverse:
        return gated_chunk_scan(q, k, v, log_f, s0)
    o, s = gated_chunk_scan(jnp.flip(q, 1), jnp.flip(k, 1), jnp.flip(v, 1), jnp.flip(log_f, 1), s0)
    return jnp.flip(o, 1), s


def hgrn_readout(o, g, norm_g, w_o, dtype):
    b, n = o.shape[:2]
    o = o * lax.rsqrt(jnp.mean(o * o, axis=-1, keepdims=True) + RMS_EPS)
    o = o * norm_g.astype(jnp.float32).reshape(HGRN_HEADS, HGRN_DV)
    o = o.reshape(b, n, -1) * jax.nn.sigmoid(g)
    return o.astype(dtype) @ w_o


def hgrn2_mixer(uc, ux, w_in, lb, norm_g, w_o, with_ctx_out):
    qc, ic, gc, dc = hgrn_project(uc, w_in, lb)
    qx, ix, gx, dx = hgrn_project(ux, w_in, lb)
    s_zero = jnp.zeros((ux.shape[0], HGRN_HEADS, HGRN_DK, HGRN_DV), jnp.float32)
    oc_f, sc_f = directional_scan(qc, dc[0][0], ic, dc[0][1], s_zero, False)
    ox_f, _ = directional_scan(qx, dx[0][0], ix, dx[0][1], sc_f, False)
    oc_b, sc_b = directional_scan(qc, dc[1][0], ic, dc[1][1], s_zero, True)
    ox_b, _ = directional_scan(qx, dx[1][0], ix, dx[1][1], sc_b, True)
    oc = hgrn_readout(oc_f + oc_b, gc, norm_g, w_o, uc.dtype) if with_ctx_out else None
    return oc, hgrn_readout(ox_f + ox_b, gx, norm_g, w_o, ux.dtype)


def swiglu(t, w13, w2):
    a, b = jnp.split(t @ w13, 2, axis=-1)
    return (jax.nn.silu(a) * b) @ w2


def moe_swiglu(t, router, w13, w2):
    shape = t.shape
    tf = t.reshape(-1, shape[-1])
    logits = (tf @ router).astype(jnp.float32)
    top_v, top_i = lax.top_k(logits, TOP_K)
    top_w = jax.nn.softmax(top_v, axis=-1)
    gates = jnp.sum(jax.nn.one_hot(top_i, N_EXPERTS, dtype=jnp.float32) * top_w[..., None], axis=1).astype(t.dtype)
    out = jnp.zeros_like(tf)
    for e in range(N_EXPERTS):
        out = out + gates[:, e:e + 1] * swiglu(tf, w13[e], w2[e])
    return out.reshape(shape)


def setup_inputs(seed: int = 0) -> dict:
    key = jax.random.key(seed)
    ks = jax.random.split(key, 24)
    D = D_MODEL
    n_attn = (DEPTH + 2) // 3
    n_fnet = (DEPTH + 1) // 3
    n_hgrn = DEPTH // 3
    n_dense = (DEPTH + 1) // 2
    n_moe = DEPTH // 2
    qkv_w = (ATTN_HEADS + 2 * ATTN_KV_HEADS) * HEAD_DIM
    hgrn_w = 3 * HGRN_HEADS * HGRN_DK + 2 * D

    def nrm(k, shape, s=1.0):
        return jax.random.normal(k, shape, jnp.float32) * s

    def w(k, shape, fan_in, gain=1.0):
        return nrm(k, shape, gain * fan_in ** -0.5)

    return {
        'x': nrm(ks[0], (BATCH, SEQ, D)),
        'c': nrm(ks[1], (BATCH, D)),
        'ctx': nrm(ks[2], (BATCH, CTX_LEN, D)),
        'c_ctx': nrm(ks[3], (D,)),
        'ada_w': w(ks[4], (DEPTH, D, 6 * D), D, 0.5),
        'ada_b': nrm(ks[5], (DEPTH, 6 * D), 0.02),
        'norm_mix': 1.0 + nrm(ks[6], (DEPTH, D), 0.02),
        'norm_ffn': 1.0 + nrm(ks[7], (DEPTH, D), 0.02),
        'norm_final': 1.0 + nrm(ks[8], (D,), 0.02),
        'attn_wqkv': w(ks[9], (n_attn, D, qkv_w), D),
        'attn_wo': w(ks[10], (n_attn, ATTN_HEADS * HEAD_DIM, D), ATTN_HEADS * HEAD_DIM),
        'attn_sink': nrm(ks[11], (n_attn, ATTN_HEADS), 0.5),
        'fnet_wo': w(ks[12], (n_fnet, D, D), D),
        'hgrn_win': w(ks[13], (n_hgrn, D, hgrn_w), D),
        'hgrn_lb': nrm(ks[14], (DEPTH, 2, HGRN_HEADS * HGRN_DK), 0.1),
        'hgrn_norm': 1.0 + nrm(ks[15], (n_hgrn, D), 0.02),
        'hgrn_wo': w(ks[16], (n_hgrn, D, D), D),
        'ffn_w13': w(ks[17], (n_dense, D, 2 * FFN_DIM), D),
        'ffn_w2': w(ks[18], (n_dense, FFN_DIM, D), FFN_DIM),
        'moe_router': w(ks[19], (n_moe, D, N_EXPERTS), D),
        'moe_w13': w(ks[20], (n_moe, N_EXPERTS, D, 2 * EXPERT_DIM), D),
        'moe_w2': w(ks[21], (n_moe, N_EXPERTS, EXPERT_DIM, D), EXPERT_DIM),
    }


def reference(x, c, ctx, c_ctx, ada_w, ada_b, norm_mix, norm_ffn, norm_final, attn_wqkv, attn_wo, attn_sink,
              fnet_wo, hgrn_win, hgrn_lb, hgrn_norm, hgrn_wo, ffn_w13, ffn_w2, moe_router, moe_w13, moe_w2):
    S = x.shape[1]
    L = ctx.shape[1]
    ang_row, ang_col = axial_rope_angles(S)
    hx, hc = x, ctx
    for i in range(DEPTH):
        with_ctx = i < DEPTH - 1
        shx1, scx1, gx1, shx2, scx2, gx2 = ada_mod(c, ada_w[i], ada_b[i])
        shc1, scc1, gc1, shc2, scc2, gc2 = ada_mod(c_ctx, ada_w[i], ada_b[i])
        ux = modulate(hx, norm_mix[i], shx1, scx1)
        uc = modulate(hc, norm_mix[i], shc1, scc1)
        kind, slot = i % N_MIXERS, i // N_MIXERS
        if kind == 0:
            oc, ox = windowed_gqa_mixer(uc, ux, attn_wqkv[slot], attn_wo[slot], attn_sink[slot],
                                        ang_row, ang_col, with_ctx)
        elif kind == 1:
            oc, ox = fourier_mixer(uc, ux, fnet_wo[slot], with_ctx)
        else:
            lb = hgrn_lower_bounds(hgrn_lb, i)
            oc, ox = hgrn2_mixer(uc, ux, hgrn_win[slot], lb, hgrn_norm[slot], hgrn_wo[slot], with_ctx)
        hx = hx + gx1 * ox
        if with_ctx:
            hc = hc + gc1 * oc

        j = i // 2
        if i % 2 == 0:
            ffn = lambda t: swiglu(t, ffn_w13[j], ffn_w2[j])
        else:
            ffn = lambda t: moe_swiglu(t, moe_router[j], moe_w13[j], moe_w2[j])
        vx = modulate(hx, norm_ffn[i], shx2, scx2)
        if with_ctx:
            vc = modulate(hc, norm_ffn[i], shc2, scc2)
            y = ffn(jnp.concatenate([vc, vx], axis=1))
            hc = hc + gc2 * y[:, :L]
            hx = hx + gx2 * y[:, L:]
        else:
            hx = hx + gx2 * ffn(vx)
    return rms_norm(hx, norm_final)
```

```python
import functools
import math

import numpy as np
import jax
import jax.numpy as jnp
from jax import lax
from jax.experimental import pallas as pl
from jax.experimental.pallas import tpu as pltpu

F32 = jnp.float32
BF16 = jnp.bfloat16

N_MIXERS = 3
RMS_EPS = 1e-6
NEG_INF = -1e30
GRID_W = 64
ATTN_HEADS = 16
ATTN_KV_HEADS = 4
ATTN_GROUP = ATTN_HEADS // ATTN_KV_HEADS
HEAD_DIM = 128
ATTN_BLOCK = 128
ROPE_THETA = 10000.0
FNET_GROUPS = 8
HGRN_HEADS = 16
HGRN_DK = 128
HGRN_DV = 128
HGRN_CHUNK = 64
N_EXPERTS = 8

LANES = 128
ROW_TILE = 512
COL_TILE = 512
W2_COL_TILE = 256
MOE_ROW_TILE = 512
COMBINE_ROW_TILE = 256
SCAN_ROWS = 128
VMEM_LIMIT_BYTES = 56 * 1024 * 1024


def _cparams(n_axes):
    return pltpu.CompilerParams(dimension_semantics=("arbitrary",) * n_axes,
                                vmem_limit_bytes=VMEM_LIMIT_BYTES)


def _dot(a, b):
    return jnp.dot(a, b, preferred_element_type=F32)


def _dot_nt(a, b):
    return lax.dot_general(a, b, (((1,), (1,)), ((), ())), preferred_element_type=F32)


def _sigmoid(x):
    return 1.0 / (1.0 + jnp.exp(-x))


def _silu(x):
    return x * _sigmoid(x)


def _ada_kernel(a_ref, w_ref, b_ref, o_ref):
    o_ref[...] = _dot(a_ref[...], w_ref[...].astype(BF16)) + b_ref[...]


def _ada_mods(cond_rows, ada_w, ada_b):
    depth, d, n6 = ada_w.shape
    rows = cond_rows.shape[0]
    tn = 1024
    return pl.pallas_call(
        _ada_kernel,
        grid=(depth, n6 // tn),
        in_specs=[pl.BlockSpec((rows, d), lambda l, n: (0, 0)),
                  pl.BlockSpec((None, d, tn), lambda l, n: (l, 0, n)),
                  pl.BlockSpec((None, 1, tn), lambda l, n: (l, 0, n))],
        out_specs=pl.BlockSpec((None, rows, tn), lambda l, n: (l, 0, n)),
        out_shape=jax.ShapeDtypeStruct((depth, rows, n6), F32),
        compiler_params=_cparams(2),
        name="ada_mods",
    )(cond_rows, ada_w, ada_b.reshape(depth, 1, n6))


def _norm_mod_kernel(h_ref, g_ref, sh_ref, sc_ref, o_ref):
    h = h_ref[...]
    y = h * lax.rsqrt(jnp.mean(h * h, axis=-1, keepdims=True) + RMS_EPS) * g_ref[...]
    o_ref[...] = (y * (1.0 + sc_ref[...]) + sh_ref[...]).astype(o_ref.dtype)


def _norm_route_kernel(h_ref, g_ref, sh_ref, sc_ref, r_ref, o_ref, route_ref):
    h = h_ref[...]
    y = h * lax.rsqrt(jnp.mean(h * h, axis=-1, keepdims=True) + RMS_EPS) * g_ref[...]
    u = y * (1.0 + sc_ref[...]) + sh_ref[...]
    o_ref[...] = u
    logits = jnp.dot(u, r_ref[...], preferred_element_type=F32, precision=lax.Precision.HIGHEST)
    lane = lax.broadcasted_iota(jnp.int32, logits.shape, 1)
    l1 = jnp.where(lane < N_EXPERTS, logits, NEG_INF)
    m1 = jnp.max(l1, axis=-1, keepdims=True)
    i1 = jnp.min(jnp.where(l1 == m1, lane, LANES), axis=-1, keepdims=True)
    l2 = jnp.where(lane == i1, NEG_INF, l1)
    m2 = jnp.max(l2, axis=-1, keepdims=True)
    i2 = jnp.min(jnp.where(l2 == m2, lane, LANES), axis=-1, keepdims=True)
    e2 = jnp.exp(m2 - m1)
    w1 = 1.0 / (1.0 + e2)
    w2 = e2 * w1
    route_ref[...] = jnp.where(lane == 0, i1.astype(F32),
                               jnp.where(lane == 1, i2.astype(F32),
                                         jnp.where(lane == 2, w1, jnp.where(lane == 3, w2, 0.0))))


def _final_norm_kernel(h_ref, g_ref, o_ref):
    h = h_ref[...]
    o_ref[...] = h * lax.rsqrt(jnp.mean(h * h, axis=-1, keepdims=True) + RMS_EPS) * g_ref[...]


def _cast_weight(w_ref, wbf_ref):
    @pl.when(pl.program_id(1) == 0)
    def _():
        wbf_ref[...] = w_ref[...].astype(BF16)


def _mm_plain_kernel(a_ref, w_ref, o_ref, wbf_ref):
    _cast_weight(w_ref, wbf_ref)
    o_ref[...] = _dot(a_ref[...], wbf_ref[...]).astype(o_ref.dtype)


def _mm_rope_kernel(a_ref, w_ref, cos_ref, sa_ref, sb_ref, o_ref, wbf_ref, *, n_q_tiles, n_rope_tiles, q_scale):
    _cast_weight(w_ref, wbf_ref)
    n = pl.program_id(0)
    acc = _dot(a_ref[...], wbf_ref[...])

    @pl.when(n >= n_rope_tiles)
    def _():
        o_ref[...] = acc.astype(o_ref.dtype)

    @pl.when(n < n_rope_tiles)
    def _():
        scale = jnp.where(n < n_q_tiles, q_scale, 1.0).astype(F32)
        cos, sa, sb = cos_ref[...], sa_ref[...], sb_ref[...]
        for c in range(acc.shape[1] // HEAD_DIM):
            t = acc[:, c * HEAD_DIM:(c + 1) * HEAD_DIM]
            r = t * cos + pltpu.roll(t, HEAD_DIM - 32, 1) * sa + pltpu.roll(t, 32, 1) * sb
            o_ref[:, c * HEAD_DIM:(c + 1) * HEAD_DIM] = (r * scale).astype(o_ref.dtype)


def _mm_swiglu_kernel(a_ref, wa_ref, wb_ref, o_ref, wbfa_ref, wbfb_ref):
    _cast_weight(wa_ref, wbfa_ref)
    _cast_weight(wb_ref, wbfb_ref)
    a = a_ref[...]
    ga = _dot(a, wbfa_ref[...])
    gb = _dot(a, wbfb_ref[...])
    o_ref[...] = (_silu(ga) * gb).astype(o_ref.dtype)


def _mm_resid_kernel(a_ref, w_ref, h_ref, g_ref, o_ref, wbf_ref):
    _cast_weight(w_ref, wbf_ref)
    o_ref[...] = h_ref[...] + g_ref[...] * _dot(a_ref[...], wbf_ref[...])


class _Rows:
    def __init__(self, batch, seq, ctx_len):
        self.batch, self.seq, self.ctx_len = batch, seq, ctx_len
        self.n_ctx = batch * ctx_len
        self.n_rows = self.n_ctx + batch * seq
        self.groups = 1 + batch
        assert self.n_ctx % ROW_TILE == 0 and seq % ROW_TILE == 0, (batch, seq, ctx_len)
        assert ctx_len % SCAN_ROWS == 0 and seq % SCAN_ROWS == 0 and ctx_len % ATTN_BLOCK == 0

    def group(self, tile, tile_rows=ROW_TILE):
        ctx_tiles = self.n_ctx // tile_rows
        per_batch = self.seq // tile_rows
        return jnp.where(tile < ctx_tiles, 0, 1 + (tile - ctx_tiles) // per_batch)


def _mod_spec(rows, layer, which, width, col_of, tile_rows=ROW_TILE):
    g = rows.groups
    return pl.BlockSpec((None, 1, width),
                        lambda n, i: ((layer * 6 + which) * g + rows.group(i, tile_rows), 0, col_of(n)))


def _mm_plain(a, w, layer, out_dtype, tn=COL_TILE, name="mm_plain"):
    m, k = a.shape
    n_total = w.shape[-1]
    return pl.pallas_call(
        _mm_plain_kernel,
        grid=(n_total // tn, m // ROW_TILE),
        in_specs=[pl.BlockSpec((ROW_TILE, k), lambda n, i: (i, 0)),
                  pl.BlockSpec((None, k, tn), lambda n, i: (layer, 0, n))],
        out_specs=pl.BlockSpec((ROW_TILE, tn), lambda n, i: (i, n)),
        out_shape=jax.ShapeDtypeStruct((m, n_total), out_dtype),
        scratch_shapes=[pltpu.VMEM((k, tn), BF16)],
        compiler_params=_cparams(2),
        name=name,
    )(a, w)


def _mm_qkv_rope(a, w, layer, tables, rows):
    m, k = a.shape
    n_total = w.shape[-1]
    tn = COL_TILE
    n_q = ATTN_HEADS * HEAD_DIM // tn
    n_rope = (ATTN_HEADS + ATTN_KV_HEADS) * HEAD_DIM // tn
    ctx_tiles = rows.n_ctx // ROW_TILE
    seq_tiles = rows.seq // ROW_TILE
    tab = lambda n, i: (jnp.where(i < ctx_tiles, i, ctx_tiles + (i - ctx_tiles) % seq_tiles), 0)
    kern = functools.partial(_mm_rope_kernel, n_q_tiles=n_q, n_rope_tiles=n_rope, q_scale=HEAD_DIM ** -0.5)
    return pl.pallas_call(
        kern,
        grid=(n_total // tn, m // ROW_TILE),
        in_specs=[pl.BlockSpec((ROW_TILE, k), lambda n, i: (i, 0)),
                  pl.BlockSpec((None, k, tn), lambda n, i: (layer, 0, n)),
                  pl.BlockSpec((ROW_TILE, HEAD_DIM), tab),
                  pl.BlockSpec((ROW_TILE, HEAD_DIM), tab),
                  pl.BlockSpec((ROW_TILE, HEAD_DIM), tab)],
        out_specs=pl.BlockSpec((ROW_TILE, tn), lambda n, i: (i, n)),
        out_shape=jax.ShapeDtypeStruct((m, n_total), BF16),
        scratch_shapes=[pltpu.VMEM((k, tn), BF16)],
        compiler_params=_cparams(2),
        name="mm_qkv_rope",
    )(a, w, *tables)


def _mm_swiglu(a, w13, layer):
    m, k = a.shape
    f = w13.shape[-1] // 2
    tn = COL_TILE
    nf = f // tn
    return pl.pallas_call(
        _mm_swiglu_kernel,
        grid=(nf, m // ROW_TILE),
        in_specs=[pl.BlockSpec((ROW_TILE, k), lambda n, i: (i, 0)),
                  pl.BlockSpec((None, k, tn), lambda n, i: (layer, 0, n)),
                  pl.BlockSpec((None, k, tn), lambda n, i: (layer, 0, n + nf))],
        out_specs=pl.BlockSpec((ROW_TILE, tn), lambda n, i: (i, n)),
        out_shape=jax.ShapeDtypeStruct((m, f), BF16),
        scratch_shapes=[pltpu.VMEM((k, tn), BF16), pltpu.VMEM((k, tn), BF16)],
        compiler_params=_cparams(2),
        name="mm_swiglu",
    )(a, w13, w13)


def _mm_resid(a, w, layer, h, mods, rows, mod_layer, which, tn, name="mm_resid"):
    m, k = a.shape
    n_total = w.shape[-1]
    return pl.pallas_call(
        _mm_resid_kernel,
        grid=(n_total // tn, m // ROW_TILE),
        in_specs=[pl.BlockSpec((ROW_TILE, k), lambda n, i: (i, 0)),
                  pl.BlockSpec((None, k, tn), lambda n, i: (layer, 0, n)),
                  pl.BlockSpec((ROW_TILE, tn), lambda n, i: (i, n)),
                  _mod_spec(rows, mod_layer, which, tn, lambda n: n)],
        out_specs=pl.BlockSpec((ROW_TILE, tn), lambda n, i: (i, n)),
        out_shape=jax.ShapeDtypeStruct((m, n_total), F32),
        scratch_shapes=[pltpu.VMEM((k, tn), BF16)],
        compiler_params=_cparams(2),
        name=name,
    )(a, w, h, mods)


def _norm_mod(h, norm_w, mods, rows, layer, which_shift):
    m, d = h.shape
    return pl.pallas_call(
        _norm_mod_kernel,
        grid=(1, m // ROW_TILE),
        in_specs=[pl.BlockSpec((ROW_TILE, d), lambda n, i: (i, 0)),
                  pl.BlockSpec((None, 1, d), lambda n, i: (layer, 0, 0)),
                  _mod_spec(rows, layer, which_shift, d, lambda n: 0),
                  _mod_spec(rows, layer, which_shift + 1, d, lambda n: 0)],
        out_specs=pl.BlockSpec((ROW_TILE, d), lambda n, i: (i, 0)),
        out_shape=jax.ShapeDtypeStruct((m, d), BF16),
        compiler_params=_cparams(2),
        name="norm_mod",
    )(h, norm_w, mods, mods)


def _norm_route(h, norm_w, mods, rows, layer, router_pad, moe_layer):
    m, d = h.shape
    return pl.pallas_call(
        _norm_route_kernel,
        grid=(1, m // ROW_TILE),
        in_specs=[pl.BlockSpec((ROW_TILE, d), lambda n, i: (i, 0)),
                  pl.BlockSpec((None, 1, d), lambda n, i: (layer, 0, 0)),
                  _mod_spec(rows, layer, 3, d, lambda n: 0),
                  _mod_spec(rows, layer, 4, d, lambda n: 0),
                  pl.BlockSpec((None, d, LANES), lambda n, i: (moe_layer, 0, 0))],
        out_specs=[pl.BlockSpec((ROW_TILE, d), lambda n, i: (i, 0)),
                   pl.BlockSpec((ROW_TILE, LANES), lambda n, i: (i, 0))],
        out_shape=[jax.ShapeDtypeStruct((m, d), F32), jax.ShapeDtypeStruct((m, LANES), F32)],
        compiler_params=_cparams(2),
        name="norm_route",
    )(h, norm_w, mods, mods, router_pad)


def _final_norm(h, norm_w, rows):
    d = h.shape[1]
    first = rows.n_ctx // ROW_TILE
    n_lat = rows.batch * rows.seq
    return pl.pallas_call(
        _final_norm_kernel,
        grid=(n_lat // ROW_TILE,),
        in_specs=[pl.BlockSpec((ROW_TILE, d), lambda i: (first + i, 0)),
                  pl.BlockSpec((1, d), lambda i: (0, 0))],
        out_specs=pl.BlockSpec((ROW_TILE, d), lambda i: (i, 0)),
        out_shape=jax.ShapeDtypeStruct((n_lat, d), F32),
        compiler_params=_cparams(1),
        name="final_norm",
    )(h, norm_w.reshape(1, d))


def _rope_tables(rows):
    s = rows.seq
    pos = np.arange(s)
    row = (pos // GRID_W).astype(np.float64)
    col = (pos % GRID_W).astype(np.float64)
    sec = HEAD_DIM // 2
    inv = ROPE_THETA ** (-np.arange(0, sec, 2, dtype=np.float64) / sec)
    inv = inv.astype(np.float32).astype(np.float64)
    ang = np.concatenate([row[:, None] * inv, row[:, None] * inv, col[:, None] * inv, col[:, None] * inv], axis=1)
    ang = ang.astype(np.float32).astype(np.float64)
    cos, sin = np.cos(ang), np.sin(ang)
    first_half = (np.arange(HEAD_DIM) % sec) < (sec // 2)
    sa = np.where(first_half[None, :], -sin, 0.0)
    sb = np.where(first_half[None, :], 0.0, sin)
    ident = np.zeros((rows.n_ctx, HEAD_DIM))
    mk = lambda ctx_rows, lat: jnp.asarray(np.concatenate([ctx_rows, lat], axis=0), dtype=F32)
    return mk(ident + 1.0, cos), mk(ident, sa), mk(ident, sb)


def _attn_block(q_ref, o_ref, sink, s_keys, v_keys, band_scores):
    q = jnp.concatenate([q_ref[:, g * HEAD_DIM:(g + 1) * HEAD_DIM] for g in range(ATTN_GROUP)], axis=0)
    s_c = _dot_nt(q, s_keys[0])
    m = jnp.maximum(jnp.max(s_c, axis=-1, keepdims=True), sink)
    if band_scores is not None:
        s_b = band_scores(_dot_nt(q, s_keys[1]))
        m = jnp.maximum(m, jnp.max(s_b, axis=-1, keepdims=True))
    p_c = jnp.exp(s_c - m)
    den = jnp.sum(p_c, axis=-1, keepdims=True) + jnp.exp(sink - m)
    o = _dot(p_c.astype(BF16), v_keys[0])
    if band_scores is not None:
        p_b = jnp.exp(s_b - m)
        den = den + jnp.sum(p_b, axis=-1, keepdims=True)
        o = o + _dot(p_b.astype(BF16), v_keys[1])
    o = o * (1.0 / den)
    for g in range(ATTN_GROUP):
        o_ref[:, g * HEAD_DIM:(g + 1) * HEAD_DIM] = o[g * ATTN_BLOCK:(g + 1) * ATTN_BLOCK].astype(o_ref.dtype)


def _attn_kernel(sink_ref, q_ref, kc_ref, vc_ref, kp_ref, ko_ref, kn_ref, vp_ref, vo_ref, vn_ref, o_ref, *,
                 slot, n_ctx_blocks, n_blocks):
    hkv = pl.program_id(1)
    step = pl.program_id(2)
    rows_q = ATTN_GROUP * ATTN_BLOCK
    grp = lax.broadcasted_iota(jnp.int32, (rows_q, 1), 0) // ATTN_BLOCK
    sink = jnp.zeros((rows_q, 1), F32)
    for g in range(ATTN_GROUP):
        sink = jnp.where(grp == g, sink_ref[slot, hkv * ATTN_GROUP + g], sink)

    @pl.when(step < n_ctx_blocks)
    def _():
        _attn_block(q_ref, o_ref, sink, (kc_ref[...],), (vc_ref[...],), None)

    @pl.when(step >= n_ctx_blocks)
    def _():
        n = step - n_ctx_blocks
        kb = jnp.concatenate([kp_ref[...], ko_ref[...], kn_ref[...]], axis=0)
        vb = jnp.concatenate([vp_ref[...], vo_ref[...], vn_ref[...]], axis=0)

        def band_scores(s_b):
            i = lax.broadcasted_iota(jnp.int32, s_b.shape, 0) & (ATTN_BLOCK - 1)
            j = lax.broadcasted_iota(jnp.int32, s_b.shape, 1)
            lo = jnp.where(n > 0, 0, ATTN_BLOCK)
            hi = jnp.where(n < n_blocks - 1, 3 * ATTN_BLOCK, 2 * ATTN_BLOCK)
            valid = (j >= i) & (j <= i + 2 * ATTN_BLOCK) & (j >= lo) & (j < hi)
            return jnp.where(valid, s_b, NEG_INF)

        _attn_block(q_ref, o_ref, sink, (kc_ref[...], kb), (vc_ref[...], vb), band_scores)


def _attention(p, sink, slot, rows):
    r = p.shape[0]
    dq = ATTN_HEADS * HEAD_DIM
    gw = ATTN_GROUP * HEAD_DIM
    kcol = dq // HEAD_DIM
    vcol = kcol + ATTN_KV_HEADS
    L, S, B = rows.ctx_len, rows.seq, rows.batch
    nb = S // ATTN_BLOCK
    nbc = L // ATTN_BLOCK
    lat0 = rows.n_ctx // ATTN_BLOCK

    def q_block(b, s):
        return jnp.where(s < nbc, b * nbc + s, lat0 + b * nb + (s - nbc))

    def band(col0, shift):
        return pl.BlockSpec((ATTN_BLOCK, HEAD_DIM),
                            lambda b, h, s: (lat0 + b * nb + jnp.clip(s - nbc + shift, 0, nb - 1), col0 + h))

    return pl.pallas_call(
        functools.partial(_attn_kernel, slot=slot, n_ctx_blocks=nbc, n_blocks=nb),
        grid=(B, ATTN_KV_HEADS, nbc + nb),
        in_specs=[pl.BlockSpec(memory_space=pltpu.SMEM),
                  pl.BlockSpec((ATTN_BLOCK, gw), lambda b, h, s: (q_block(b, s), h)),
                  pl.BlockSpec((L, HEAD_DIM), lambda b, h, s: (b, kcol + h)),
                  pl.BlockSpec((L, HEAD_DIM), lambda b, h, s: (b, vcol + h)),
                  band(kcol, -1), band(kcol, 0), band(kcol, 1), band(vcol, -1), band(vcol, 0), band(vcol, 1)],
        out_specs=pl.BlockSpec((ATTN_BLOCK, gw), lambda b, h, s: (q_block(b, s), h)),
        out_shape=jax.ShapeDtypeStruct((r, dq), BF16),
        compiler_params=_cparams(3),
        name="attention",
    )(sink, p, p, p, p, p, p, p, p, p)


def _dft_cs(n, scale):
    k = np.arange(n)
    ang = 2.0 * np.pi * ((k[:, None] * k[None, :]) % n) / n
    return np.cos(ang) * scale, np.sin(ang) * scale


def _chan_dft_kernel(u_ref, m_ref, o_ref, *, gd):
    for g in range(u_ref.shape[1] // gd):
        v = _dot(u_ref[:, g * gd:(g + 1) * gd], m_ref[...])
        o_ref[0, :, g * gd:(g + 1) * gd] = v[:, :gd].astype(o_ref.dtype)
        o_ref[1, :, g * gd:(g + 1) * gd] = v[:, gd:].astype(o_ref.dtype)


def _chan_dft(u, row0, n_rows, mat):
    d = u.shape[1]
    gd = d // FNET_GROUPS
    t0 = row0 // ROW_TILE
    return pl.pallas_call(
        functools.partial(_chan_dft_kernel, gd=gd),
        grid=(n_rows // ROW_TILE,),
        in_specs=[pl.BlockSpec((ROW_TILE, d), lambda i: (t0 + i, 0)),
                  pl.BlockSpec((gd, 2 * gd), lambda i: (0, 0))],
        out_specs=pl.BlockSpec((2, ROW_TILE, d), lambda i: (0, i, 0)),
        out_shape=jax.ShapeDtypeStruct((2, n_rows, d), BF16),
        compiler_params=_cparams(1),
        name="fnet_chan_dft",
    )(u, mat)


def _seq_dft_a_kernel(x_ref, m_ref, tc_ref, ts_ref, o_ref):
    n1 = x_ref.shape[1]
    x = jnp.concatenate([x_ref[0], x_ref[1]], axis=0)
    z = _dot(m_ref[...], x)
    zr, zi = z[:n1], z[n1:]
    tc, ts = tc_ref[...], ts_ref[...]
    for c in range(z.shape[1] // LANES):
        sl = slice(c * LANES, (c + 1) * LANES)
        o_ref[0, :, sl] = (zr[:, sl] * tc + zi[:, sl] * ts).astype(o_ref.dtype)
        o_ref[1, :, sl] = (zi[:, sl] * tc - zr[:, sl] * ts).astype(o_ref.dtype)


def _seq_dft_c_kernel(x_ref, m_ref, o_ref):
    x = jnp.concatenate([x_ref[0], x_ref[1]], axis=0)
    o_ref[...] = _dot(m_ref[...], x).astype(o_ref.dtype)


def _fourier_tokens(u, rows):
    d = u.shape[1]
    gd = d // FNET_GROUPS
    B, S, L = rows.batch, rows.seq, rows.ctx_len
    n2 = GRID_W
    n1 = S // n2
    cc, sc = _dft_cs(gd, gd ** -0.5)
    chan = jnp.asarray(np.concatenate([cc, -sc], axis=1), dtype=BF16)

    v = _chan_dft(u, rows.n_ctx, B * S, chan)
    v = v.reshape(2, B, n1, n2 * d)
    c1, s1 = _dft_cs(n1, n1 ** -0.5)
    m1 = jnp.asarray(np.block([[c1, s1], [-s1, c1]]), dtype=BF16)
    ang = 2.0 * np.pi * (np.arange(n2)[:, None] * np.arange(n1)[None, :]) / S
    tw_c = jnp.asarray(np.broadcast_to(np.cos(ang)[:, :, None], (n2, n1, LANES)), dtype=F32)
    tw_s = jnp.asarray(np.broadcast_to(np.sin(ang)[:, :, None], (n2, n1, LANES)), dtype=F32)
    z = pl.pallas_call(
        _seq_dft_a_kernel,
        grid=(B, n2),
        in_specs=[pl.BlockSpec((2, None, n1, d), lambda b, t: (0, b, 0, t)),
                  pl.BlockSpec((2 * n1, 2 * n1), lambda b, t: (0, 0)),
                  pl.BlockSpec((None, n1, LANES), lambda b, t: (t, 0, 0)),
                  pl.BlockSpec((None, n1, LANES), lambda b, t: (t, 0, 0))],
        out_specs=pl.BlockSpec((2, None, None, n1, d), lambda b, t: (0, b, t, 0, 0)),
        out_shape=jax.ShapeDtypeStruct((2, B, n2, n1, d), BF16),
        compiler_params=_cparams(2),
        name="fnet_seq_dft_a",
    )(v, m1, tw_c, tw_s)
    z = z.reshape(2, B, n2, n1 * d)
    c2, s2 = _dft_cs(n2, n2 ** -0.5)
    m2 = jnp.asarray(np.concatenate([c2, s2], axis=1), dtype=BF16)
    tc = 8192
    assert (n1 * d) % tc == 0
    y_lat = pl.pallas_call(
        _seq_dft_c_kernel,
        grid=(B, n1 * d // tc),
        in_specs=[pl.BlockSpec((2, None, n2, tc), lambda b, j: (0, b, 0, j)),
                  pl.BlockSpec((n2, 2 * n2), lambda b, j: (0, 0))],
        out_specs=pl.BlockSpec((None, n2, tc), lambda b, j: (b, 0, j)),
        out_shape=jax.ShapeDtypeStruct((B, n2, n1 * d), BF16),
        compiler_params=_cparams(2),
        name="fnet_seq_dft_c",
    )(z, m2)

    vc = _chan_dft(u, 0, B * L, chan).reshape(2, B, L, d)
    cl, sl = _dft_cs(L, L ** -0.5)
    ml = jnp.asarray(np.concatenate([cl, sl], axis=1), dtype=BF16)
    y_ctx = pl.pallas_call(
        _seq_dft_c_kernel,
        grid=(B, 1),
        in_specs=[pl.BlockSpec((2, None, L, d), lambda b, j: (0, b, 0, 0)),
                  pl.BlockSpec((L, 2 * L), lambda b, j: (0, 0))],
        out_specs=pl.BlockSpec((None, L, d), lambda b, j: (b, 0, 0)),
        out_shape=jax.ShapeDtypeStruct((B, L, d), BF16),
        compiler_params=_cparams(2),
        name="fnet_ctx_dft",
    )(vc, ml)
    return jnp.concatenate([y_ctx.reshape(B * L, d), y_lat.reshape(B * S, d)], axis=0)


def _hgrn_scan_kernel(q_ref, f_ref, v_ref, lb_ref, tri_ref, o_ref, st_ref, *, reverse):
    @pl.when(pl.program_id(1) == 0)
    def _():
        st_ref[...] = jnp.zeros(st_ref.shape, st_ref.dtype)

    c = HGRN_CHUNK
    n_sub = q_ref.shape[0] // c
    lb = lb_ref[...]
    tri = tri_ref[...]
    ti = lax.broadcasted_iota(jnp.int32, (c, c), 0)
    si = lax.broadcasted_iota(jnp.int32, (c, c), 1)
    keep = (si >= ti) if reverse else (si <= ti)
    order = range(n_sub - 1, -1, -1) if reverse else range(n_sub)
    for sub in order:
        rs = slice(sub * c, (sub + 1) * c)
        q = _silu(q_ref[rs, :])
        f = lb + (1.0 - lb) * _sigmoid(f_ref[rs, :])
        k = 1.0 - f
        bsum = jnp.dot(tri, jnp.log(f), preferred_element_type=F32, precision=lax.Precision.HIGHEST)
        b_end = bsum[0:1, :] if reverse else bsum[c - 1:c, :]
        q_in = (q * jnp.exp(bsum)).astype(BF16)
        k_in = (k * jnp.exp(-bsum)).astype(BF16)
        k_out = k * jnp.exp(b_end - bsum)
        decay = jnp.exp(b_end)
        v = v_ref[rs, :]
        for h in range(HGRN_HEADS):
            ks = slice(h * HGRN_DK, (h + 1) * HGRN_DK)
            vs = slice(h * HGRN_DV, (h + 1) * HGRN_DV)
            a = jnp.where(keep, _dot_nt(q_in[:, ks], k_in[:, ks]), 0.0)
            vh = v[:, vs]
            st = st_ref[h]
            o = _dot(a.astype(BF16), vh.astype(BF16)) + _dot_nt(q_in[:, ks], st.astype(BF16))
            o_ref[rs, vs] = o
            st_ref[h] = st * decay[:, ks] + _dot(vh.T.astype(BF16), k_out[:, ks].astype(BF16))


def _hgrn_scan(pm, lb, rows, reverse):
    r = pm.shape[0]
    hk = HGRN_HEADS * HGRN_DK
    B, S, L = rows.batch, rows.seq, rows.ctx_len
    cs, ls = L // SCAN_ROWS, S // SCAN_ROWS
    direction = 1 if reverse else 0

    def row_block(b, s):
        if reverse:
            ctx = b * cs + (cs - 1 - s)
            lat = B * cs + b * ls + (ls - 1 - (s - cs))
        else:
            ctx = b * cs + s
            lat = B * cs + b * ls + (s - cs)
        return jnp.where(s < cs, ctx, lat)

    c = HGRN_CHUNK
    tri_np = np.triu(np.ones((c, c))) if reverse else np.tril(np.ones((c, c)))
    tri = jnp.asarray(tri_np, dtype=F32)
    col = lambda j: pl.BlockSpec((SCAN_ROWS, hk), lambda b, s: (row_block(b, s), j))
    return pl.pallas_call(
        functools.partial(_hgrn_scan_kernel, reverse=reverse),
        grid=(B, cs + ls),
        in_specs=[col(0), col(1 + direction), col(3),
                  pl.BlockSpec((None, 1, hk), lambda b, s: (direction, 0, 0)),
                  pl.BlockSpec((c, c), lambda b, s: (0, 0))],
        out_specs=pl.BlockSpec((SCAN_ROWS, HGRN_HEADS * HGRN_DV), lambda b, s: (row_block(b, s), 0)),
        out_shape=jax.ShapeDtypeStruct((r, HGRN_HEADS * HGRN_DV), F32),
        scratch_shapes=[pltpu.VMEM((HGRN_HEADS, HGRN_DV, HGRN_DK), F32)],
        compiler_params=_cparams(2),
        name="hgrn_scan_bwd" if reverse else "hgrn_scan_fwd",
    )(pm, pm, pm, lb, tri)


def _hgrn_readout_kernel(of_ref, ob_ref, g_ref, ng_ref, o_ref):
    for h in range(HGRN_HEADS):
        vs = slice(h * HGRN_DV, (h + 1) * HGRN_DV)
        o = of_ref[:, vs] + ob_ref[:, vs]
        o = o * lax.rsqrt(jnp.mean(o * o, axis=-1, keepdims=True) + RMS_EPS) * ng_ref[:, vs]
        o_ref[:, vs] = (o * _sigmoid(g_ref[:, vs])).astype(o_ref.dtype)


def _hgrn_readout(o_f, o_b, pm, norm_g, slot):
    r, d = o_f.shape
    gcol = pm.shape[1] // d - 1
    return pl.pallas_call(
        _hgrn_readout_kernel,
        grid=(r // ROW_TILE,),
        in_specs=[pl.BlockSpec((ROW_TILE, d), lambda i: (i, 0)),
                  pl.BlockSpec((ROW_TILE, d), lambda i: (i, 0)),
                  pl.BlockSpec((ROW_TILE, d), lambda i: (i, gcol)),
                  pl.BlockSpec((None, 1, d), lambda i: (slot, 0, 0))],
        out_specs=pl.BlockSpec((ROW_TILE, d), lambda i: (i, 0)),
        out_shape=jax.ShapeDtypeStruct((r, d), BF16),
        compiler_params=_cparams(1),
        name="hgrn_readout",
    )(o_f, o_b, pm, norm_g)


def _moe_plan(route, n_rows, n_tiles):
    e_flat = jnp.concatenate([route[:, 0], route[:, 1]]).astype(jnp.int32)
    w_flat = jnp.concatenate([route[:, 2], route[:, 3]])
    t_flat = jnp.tile(jnp.arange(n_rows, dtype=jnp.int32), 2)
    onehot = (e_flat[:, None] == jnp.arange(N_EXPERTS, dtype=jnp.int32)[None, :]).astype(jnp.int32)
    csum = jnp.cumsum(onehot, axis=0)
    counts = csum[-1]
    rank = jnp.sum(csum * onehot, axis=1) - 1
    padded = ((counts + MOE_ROW_TILE - 1) // MOE_ROW_TILE) * MOE_ROW_TILE
    ends = jnp.cumsum(padded)
    starts = ends - padded
    dest = jnp.sum(starts[None, :] * onehot, axis=1) + rank
    p_rows = n_tiles * MOE_ROW_TILE
    row_token = jnp.zeros((p_rows,), jnp.int32).at[dest].set(t_flat)
    row_w = jnp.zeros((p_rows,), F32).at[dest].set(w_flat)
    n_used = (ends[-1] // MOE_ROW_TILE).astype(jnp.int32)
    tile = jnp.arange(n_tiles, dtype=jnp.int32)
    te = jnp.sum((ends[None, :] <= (tile * MOE_ROW_TILE)[:, None]).astype(jnp.int32), axis=1)
    te = jnp.minimum(te, N_EXPERTS - 1)
    te_last = jnp.sum(jnp.where(tile == n_used - 1, te, 0))
    te = jnp.where(tile < n_used, te, te_last).astype(jnp.int32)
    return row_token, row_w.reshape(p_rows, 1), dest, te, n_used.reshape(1)


def _moe_gather_kernel(tok_ref, src_ref, o_ref, buf_ref, sem):
    base = pl.program_id(0) * MOE_ROW_TILE

    def row_copy(r, t):
        return pltpu.make_async_copy(src_ref.at[pl.ds(t, 1), :], buf_ref.at[pl.ds(r, 1), :], sem)

    def issue(r, carry):
        row_copy(r, tok_ref[base + r]).start()
        return carry

    def drain(r, carry):
        row_copy(r, 0).wait()
        return carry

    lax.fori_loop(0, MOE_ROW_TILE, issue, 0, unroll=8)
    lax.fori_loop(0, MOE_ROW_TILE, drain, 0, unroll=8)
    o_ref[...] = buf_ref[...].astype(o_ref.dtype)


def _moe_gather(row_token, u, n_tiles):
    d = u.shape[1]
    return pl.pallas_call(
        _moe_gather_kernel,
        grid_spec=pltpu.PrefetchScalarGridSpec(
            num_scalar_prefetch=1,
            grid=(n_tiles,),
            in_specs=[pl.BlockSpec(memory_space=pl.ANY)],
            out_specs=pl.BlockSpec((MOE_ROW_TILE, d), lambda i, tok: (i, 0)),
            scratch_shapes=[pltpu.VMEM((MOE_ROW_TILE, d), F32), pltpu.SemaphoreType.DMA(())]),
        out_shape=jax.ShapeDtypeStruct((n_tiles * MOE_ROW_TILE, d), BF16),
        compiler_params=_cparams(1),
        name="moe_gather",
    )(row_token, u)


def _expert_changed(te_ref):
    i = pl.program_id(1)
    return (i == 0) | (te_ref[i] != te_ref[jnp.maximum(i - 1, 0)])


def _gmm_swiglu_kernel(te_ref, nu_ref, x_ref, wa_ref, wb_ref, o_ref, wbfa_ref, wbfb_ref):
    i = pl.program_id(1)

    @pl.when(_expert_changed(te_ref))
    def _():
        wbfa_ref[...] = wa_ref[...].astype(BF16)
        wbfb_ref[...] = wb_ref[...].astype(BF16)

    @pl.when(i < nu_ref[0])
    def _():
        x = x_ref[...]
        o_ref[...] = (_silu(_dot(x, wbfa_ref[...])) * _dot(x, wbfb_ref[...])).astype(o_ref.dtype)

    @pl.when(i >= nu_ref[0])
    def _():
        o_ref[...] = jnp.zeros(o_ref.shape, o_ref.dtype)


def _gmm_scale_kernel(te_ref, nu_ref, x_ref, w_ref, rw_ref, o_ref, wbf_ref):
    i = pl.program_id(1)

    @pl.when(_expert_changed(te_ref))
    def _():
        wbf_ref[...] = w_ref[...].astype(BF16)

    @pl.when(i < nu_ref[0])
    def _():
        o_ref[...] = rw_ref[...] * _dot(x_ref[...], wbf_ref[...])

    @pl.when(i >= nu_ref[0])
    def _():
        o_ref[...] = jnp.zeros(o_ref.shape, o_ref.dtype)


def _moe_experts(xg, row_w, te, n_used, w13, w2, layer, n_tiles):
    p_rows, d = xg.shape
    f = w13.shape[-1] // 2
    tn = COL_TILE
    nf = f // tn
    used = lambda i, nu: jnp.minimum(i, nu[0] - 1)
    hidden = pl.pallas_call(
        _gmm_swiglu_kernel,
        grid_spec=pltpu.PrefetchScalarGridSpec(
            num_scalar_prefetch=2,
            grid=(nf, n_tiles),
            in_specs=[pl.BlockSpec((MOE_ROW_TILE, d), lambda n, i, te, nu: (used(i, nu), 0)),
                      pl.BlockSpec((None, None, d, tn), lambda n, i, te, nu: (layer, te[i], 0, n)),
                      pl.BlockSpec((None, None, d, tn), lambda n, i, te, nu: (layer, te[i], 0, n + nf))],
            out_specs=pl.BlockSpec((MOE_ROW_TILE, tn), lambda n, i, te, nu: (i, n)),
            scratch_shapes=[pltpu.VMEM((d, tn), BF16), pltpu.VMEM((d, tn), BF16)]),
        out_shape=jax.ShapeDtypeStruct((p_rows, f), BF16),
        compiler_params=_cparams(2),
        name="moe_w13",
    )(te, n_used, xg, w13, w13)
    tn2 = W2_COL_TILE
    return pl.pallas_call(
        _gmm_scale_kernel,
        grid_spec=pltpu.PrefetchScalarGridSpec(
            num_scalar_prefetch=2,
            grid=(d // tn2, n_tiles),
            in_specs=[pl.BlockSpec((MOE_ROW_TILE, f), lambda n, i, te, nu: (used(i, nu), 0)),
                      pl.BlockSpec((None, None, f, tn2), lambda n, i, te, nu: (layer, te[i], 0, n)),
                      pl.BlockSpec((MOE_ROW_TILE, 1), lambda n, i, te, nu: (i, 0))],
            out_specs=pl.BlockSpec((MOE_ROW_TILE, tn2), lambda n, i, te, nu: (i, n)),
            scratch_shapes=[pltpu.VMEM((f, tn2), BF16)]),
        out_shape=jax.ShapeDtypeStruct((p_rows, d), F32),
        compiler_params=_cparams(2),
        name="moe_w2",
    )(te, n_used, hidden, w2, row_w)


def _moe_combine_kernel(p1_ref, p2_ref, y_ref, h_ref, g_ref, o_ref, buf_ref, sem):
    base = pl.program_id(0) * COMBINE_ROW_TILE

    def row_copy(k, r, p):
        return pltpu.make_async_copy(y_ref.at[pl.ds(p, 1), :], buf_ref.at[k, pl.ds(r, 1), :], sem)

    def issue(r, carry):
        row_copy(0, r, p1_ref[base + r]).start()
        row_copy(1, r, p2_ref[base + r]).start()
        return carry

    def drain(r, carry):
        row_copy(0, r, 0).wait()
        row_copy(1, r, 0).wait()
        return carry

    lax.fori_loop(0, COMBINE_ROW_TILE, issue, 0, unroll=8)
    lax.fori_loop(0, COMBINE_ROW_TILE, drain, 0, unroll=8)
    o_ref[...] = h_ref[...] + g_ref[...] * (buf_ref[0] + buf_ref[1])


def _moe_combine(pos1, pos2, y, h, mods, rows, mod_layer):
    r, d = h.shape
    g = rows.groups
    tr = COMBINE_ROW_TILE
    return pl.pallas_call(
        _moe_combine_kernel,
        grid_spec=pltpu.PrefetchScalarGridSpec(
            num_scalar_prefetch=2,
            grid=(r // tr,),
            in_specs=[pl.BlockSpec(memory_space=pl.ANY),
                      pl.BlockSpec((tr, d), lambda i, p1, p2: (i, 0)),
                      pl.BlockSpec((None, 1, d),
                                   lambda i, p1, p2: ((mod_layer * 6 + 5) * g + rows.group(i, tr), 0, 0))],
            out_specs=pl.BlockSpec((tr, d), lambda i, p1, p2: (i, 0)),
            scratch_shapes=[pltpu.VMEM((2, tr, d), F32), pltpu.SemaphoreType.DMA(())]),
        out_shape=jax.ShapeDtypeStruct((r, d), F32),
        compiler_params=_cparams(1),
        name="moe_combine",
    )(pos1, pos2, y, h, mods)


def _moe_ffn(h, norm_w, mods, rows, layer, router_pad, w13, w2, moe_layer):
    r = h.shape[0]
    n_tiles = -(-(2 * r + N_EXPERTS * (MOE_ROW_TILE - 1)) // MOE_ROW_TILE)
    u, route = _norm_route(h, norm_w, mods, rows, layer, router_pad, moe_layer)
    row_token, row_w, dest, te, n_used = _moe_plan(route, r, n_tiles)
    xg = _moe_gather(row_token, u, n_tiles)
    y = _moe_experts(xg, row_w, te, n_used, w13, w2, moe_layer, n_tiles)
    return _moe_combine(dest[:r], dest[r:], y, h, mods, rows, layer)


def _hgrn_lower_bounds(lb_logits, layer):
    gamma = jax.nn.softmax(lb_logits.astype(F32), axis=0)
    lb = jnp.cumsum(gamma, axis=0) - gamma[0]
    return lb[layer]


def kernel(x, c, ctx, c_ctx, ada_w, ada_b, norm_mix, norm_ffn, norm_final, attn_wqkv, attn_wo, attn_sink, fnet_wo,
           hgrn_win, hgrn_lb, hgrn_norm, hgrn_wo, ffn_w13, ffn_w2, moe_router, moe_w13, moe_w2):
    B, S, D = x.shape
    L = ctx.shape[1]
    depth = ada_w.shape[0]
    rows = _Rows(B, S, L)
    G = rows.groups

    cond = jnp.concatenate([c_ctx[None, :], c], axis=0)
    cond = jnp.pad(_silu(cond), ((0, 16 - G), (0, 0))).astype(BF16)
    mods = _ada_mods(cond, ada_w, ada_b)
    mods = mods[:, :G, :].reshape(depth, G, 6, D).transpose(0, 2, 1, 3).reshape(depth * 6 * G, 1, D)

    h = jnp.concatenate([ctx.reshape(B * L, D), x.reshape(B * S, D)], axis=0)
    norm_mix3 = norm_mix.reshape(depth, 1, D)
    norm_ffn3 = norm_ffn.reshape(depth, 1, D)
    rope_tables = _rope_tables(rows)
    router_pad = jnp.pad(moe_router, ((0, 0), (0, 0), (0, LANES - moe_router.shape[-1])))

    for i in range(depth):
        kind, slot = i % N_MIXERS, i // N_MIXERS
        u = _norm_mod(h, norm_mix3, mods, rows, i, 0)
        if kind == 0:
            p = _mm_qkv_rope(u, attn_wqkv, slot, rope_tables, rows)
            o = _attention(p, attn_sink, slot, rows)
            h = _mm_resid(o, attn_wo, slot, h, mods, rows, i, 2, COL_TILE, name="attn_out")
        elif kind == 1:
            y = _fourier_tokens(u, rows)
            h = _mm_resid(y, fnet_wo, slot, h, mods, rows, i, 2, COL_TILE, name="fnet_out")
        else:
            lb = _hgrn_lower_bounds(hgrn_lb, i).reshape(2, 1, HGRN_HEADS * HGRN_DK)
            pm = _mm_plain(u, hgrn_win, slot, F32, name="hgrn_in")
            o_f = _hgrn_scan(pm, lb, rows, reverse=False)
            o_b = _hgrn_scan(pm, lb, rows, reverse=True)
            o = _hgrn_readout(o_f, o_b, pm, hgrn_norm.reshape(-1, 1, D), slot)
            h = _mm_resid(o, hgrn_wo, slot, h, mods, rows, i, 2, COL_TILE, name="hgrn_out")

        j = i // 2
        if i % 2 == 0:
            v = _norm_mod(h, norm_ffn3, mods, rows, i, 3)
            g = _mm_swiglu(v, ffn_w13, j)
            h = _mm_resid(g, ffn_w2, j, h, mods, rows, i, 5, W2_COL_TILE, name="ffn_out")
        else:
            h = _moe_ffn(h, norm_ffn3, mods, rows, i, router_pad, moe_w13, moe_w2, j)

    return _final_norm(h, norm_final, rows).reshape(B, S, D)
```

```python
import functools

import numpy as np
import jax
import jax.numpy as jnp
from jax import lax
from jax.experimental import pallas as pl
from jax.experimental.pallas import tpu as pltpu

F32 = jnp.float32
BF16 = jnp.bfloat16

N_MIXERS = 3
RMS_EPS = 1e-6
NEG_INF = -1e30
GRID_W = 64
ATTN_HEADS = 16
ATTN_KV_HEADS = 4
ATTN_GROUP = ATTN_HEADS // ATTN_KV_HEADS
HEAD_DIM = 128
ATTN_BLOCK = 128
ROPE_THETA = 10000.0
FNET_GROUPS = 8
HGRN_HEADS = 16
HGRN_DK = 128
HGRN_DV = 128
HGRN_CHUNK = 64
N_EXPERTS = 8

LANES = 128
ROW_TILE = 512
COL_TILE = 512
FFN_W2_COL_TILE = 1024
MOE_W2_COL_TILE = 1024
MOE_ROW_TILE = 512
COMBINE_ROW_TILE = 256
SCAN_ROWS = 128
VMEM_LIMIT_BYTES = 56 * 1024 * 1024


def _cparams(n_axes):
    return pltpu.CompilerParams(dimension_semantics=("arbitrary",) * n_axes,
                                vmem_limit_bytes=VMEM_LIMIT_BYTES)


def _dot(a, b):
    return jnp.dot(a, b, preferred_element_type=F32)


def _dot_nt(a, b):
    return lax.dot_general(a, b, (((1,), (1,)), ((), ())), preferred_element_type=F32)


def _sigmoid(x):
    return 1.0 / (1.0 + jnp.exp(-x))


def _silu(x):
    return x * _sigmoid(x)


def _ada_kernel(a_ref, w_ref, b_ref, o_ref):
    o_ref[...] = _dot(a_ref[...], w_ref[...].astype(BF16)) + b_ref[...]


def _ada_mods(cond_rows, ada_w, ada_b):
    depth, d, n6 = ada_w.shape
    rows = cond_rows.shape[0]
    tn = 1024
    return pl.pallas_call(
        _ada_kernel,
        grid=(depth, n6 // tn),
        in_specs=[pl.BlockSpec((rows, d), lambda l, n: (0, 0)),
                  pl.BlockSpec((None, d, tn), lambda l, n: (l, 0, n)),
                  pl.BlockSpec((None, 1, tn), lambda l, n: (l, 0, n))],
        out_specs=pl.BlockSpec((None, rows, tn), lambda l, n: (l, 0, n)),
        out_shape=jax.ShapeDtypeStruct((depth, rows, n6), F32),
        compiler_params=_cparams(2),
        name="ada_mods",
    )(cond_rows, ada_w, ada_b.reshape(depth, 1, n6))


def _norm_mod_kernel(h_ref, g_ref, sh_ref, sc_ref, o_ref):
    h = h_ref[...]
    y = h * lax.rsqrt(jnp.mean(h * h, axis=-1, keepdims=True) + RMS_EPS) * g_ref[...]
    o_ref[...] = (y * (1.0 + sc_ref[...]) + sh_ref[...]).astype(o_ref.dtype)


def _norm_route_kernel(h_ref, g_ref, sh_ref, sc_ref, r_ref, o_ref, route_ref):
    h = h_ref[...]
    y = h * lax.rsqrt(jnp.mean(h * h, axis=-1, keepdims=True) + RMS_EPS) * g_ref[...]
    u = y * (1.0 + sc_ref[...]) + sh_ref[...]
    o_ref[...] = u
    logits = jnp.dot(u, r_ref[...], preferred_element_type=F32, precision=lax.Precision.HIGHEST)
    lane = lax.broadcasted_iota(jnp.int32, logits.shape, 1)
    l1 = jnp.where(lane < N_EXPERTS, logits, NEG_INF)
    m1 = jnp.max(l1, axis=-1, keepdims=True)
    i1 = jnp.min(jnp.where(l1 == m1, lane, LANES), axis=-1, keepdims=True)
    l2 = jnp.where(lane == i1, NEG_INF, l1)
    m2 = jnp.max(l2, axis=-1, keepdims=True)
    i2 = jnp.min(jnp.where(l2 == m2, lane, LANES), axis=-1, keepdims=True)
    e2 = jnp.exp(m2 - m1)
    w1 = 1.0 / (1.0 + e2)
    w2 = e2 * w1
    route_ref[...] = jnp.where(lane == 0, i1.astype(F32),
                               jnp.where(lane == 1, i2.astype(F32),
                                         jnp.where(lane == 2, w1, jnp.where(lane == 3, w2, 0.0))))


def _final_norm_kernel(h_ref, g_ref, o_ref):
    h = h_ref[...]
    o_ref[...] = h * lax.rsqrt(jnp.mean(h * h, axis=-1, keepdims=True) + RMS_EPS) * g_ref[...]


def _cast_weight(w_ref, wbf_ref):
    @pl.when(pl.program_id(1) == 0)
    def _():
        wbf_ref[...] = w_ref[...].astype(BF16)


def _mm_rope_kernel(a_ref, w_ref, cos_ref, sa_ref, sb_ref, o_ref, wbf_ref, *, n_q_tiles, n_rope_tiles, q_scale):
    _cast_weight(w_ref, wbf_ref)
    n = pl.program_id(0)
    acc = _dot(a_ref[...], wbf_ref[...])

    @pl.when(n >= n_rope_tiles)
    def _():
        o_ref[...] = acc.astype(o_ref.dtype)

    @pl.when(n < n_rope_tiles)
    def _():
        scale = jnp.where(n < n_q_tiles, q_scale, 1.0).astype(F32)
        cos, sa, sb = cos_ref[...], sa_ref[...], sb_ref[...]
        for c in range(acc.shape[1] // HEAD_DIM):
            t = acc[:, c * HEAD_DIM:(c + 1) * HEAD_DIM]
            r = t * cos + pltpu.roll(t, HEAD_DIM - 32, 1) * sa + pltpu.roll(t, 32, 1) * sb
            o_ref[:, c * HEAD_DIM:(c + 1) * HEAD_DIM] = (r * scale).astype(o_ref.dtype)


def _mm_swiglu_kernel(a_ref, wa_ref, wb_ref, w2_ref, o_ref, w2bf_ref, wbfa_ref, wbfb_ref):
    _cast_weight(wa_ref, wbfa_ref)
    _cast_weight(wb_ref, wbfb_ref)
    a = a_ref[...]
    ga = _dot(a, wbfa_ref[...])
    gb = _dot(a, wbfb_ref[...])
    o_ref[...] = (_silu(ga) * gb).astype(o_ref.dtype)
    w2bf_ref[...] = w2_ref[...].astype(BF16)


def _mm_hgrn_in_kernel(a_ref, w_ref, o_ref, wbf_ref, *, tiles_per_segment):
    _cast_weight(w_ref, wbf_ref)
    seg = pl.program_id(0) // tiles_per_segment
    acc = _dot(a_ref[...], wbf_ref[...])

    @pl.when(seg == 0)
    def _():
        o_ref[...] = _silu(acc).astype(o_ref.dtype)

    @pl.when(seg == 4)
    def _():
        o_ref[...] = _sigmoid(acc).astype(o_ref.dtype)

    @pl.when((seg != 0) & (seg != 4))
    def _():
        o_ref[...] = acc.astype(o_ref.dtype)


def _mm_resid_kernel(a_ref, w_ref, h_ref, g_ref, o_ref, wbf_ref):
    _cast_weight(w_ref, wbf_ref)
    o_ref[...] = h_ref[...] + g_ref[...] * _dot(a_ref[...], wbf_ref[...])


def _mm_resid_bf16w_kernel(a_ref, w_ref, h_ref, g_ref, o_ref):
    o_ref[...] = h_ref[...] + g_ref[...] * _dot(a_ref[...], w_ref[...])


def _side_cast_rows(total_rows, n_steps):
    for rows in range(16, total_rows + 1, 16):
        if total_rows % rows == 0 and total_rows // rows <= n_steps:
            return rows
    raise ValueError((total_rows, n_steps))


class _Rows:
    def __init__(self, batch, seq, ctx_len):
        self.batch, self.seq, self.ctx_len = batch, seq, ctx_len
        self.n_ctx = batch * ctx_len
        self.n_rows = self.n_ctx + batch * seq
        self.groups = 1 + batch
        assert self.n_ctx % ROW_TILE == 0 and seq % ROW_TILE == 0, (batch, seq, ctx_len)
        assert ctx_len % SCAN_ROWS == 0 and seq % SCAN_ROWS == 0 and ctx_len % ATTN_BLOCK == 0

    def group(self, tile, tile_rows=ROW_TILE):
        ctx_tiles = self.n_ctx // tile_rows
        per_batch = self.seq // tile_rows
        return jnp.where(tile < ctx_tiles, 0, 1 + (tile - ctx_tiles) // per_batch)


def _mod_spec(rows, layer, which, width, col_of, tile_rows=ROW_TILE):
    g = rows.groups
    return pl.BlockSpec((None, 1, width),
                        lambda n, i: ((layer * 6 + which) * g + rows.group(i, tile_rows), 0, col_of(n)))


def _mm_qkv_rope(a, w, layer, tables, rows):
    m, k = a.shape
    n_total = w.shape[-1]
    tn = COL_TILE
    n_q = ATTN_HEADS * HEAD_DIM // tn
    n_rope = (ATTN_HEADS + ATTN_KV_HEADS) * HEAD_DIM // tn
    ctx_tiles = rows.n_ctx // ROW_TILE
    seq_tiles = rows.seq // ROW_TILE
    tab = lambda n, i: (jnp.where(i < ctx_tiles, i, ctx_tiles + (i - ctx_tiles) % seq_tiles), 0)
    kern = functools.partial(_mm_rope_kernel, n_q_tiles=n_q, n_rope_tiles=n_rope, q_scale=HEAD_DIM ** -0.5)
    return pl.pallas_call(
        kern,
        grid=(n_total // tn, m // ROW_TILE),
        in_specs=[pl.BlockSpec((ROW_TILE, k), lambda n, i: (i, 0)),
                  pl.BlockSpec((None, k, tn), lambda n, i: (layer, 0, n)),
                  pl.BlockSpec((ROW_TILE, HEAD_DIM), tab),
                  pl.BlockSpec((ROW_TILE, HEAD_DIM), tab),
                  pl.BlockSpec((ROW_TILE, HEAD_DIM), tab)],
        out_specs=pl.BlockSpec((ROW_TILE, tn), lambda n, i: (i, n)),
        out_shape=jax.ShapeDtypeStruct((m, n_total), BF16),
        scratch_shapes=[pltpu.VMEM((k, tn), BF16)],
        compiler_params=_cparams(2),
        name="mm_qkv_rope",
    )(a, w, *tables)


def _mm_swiglu(a, w13, w2, layer):
    m, k = a.shape
    f = w13.shape[-1] // 2
    d_out = w2.shape[-1]
    tn = COL_TILE
    nf = f // tn
    n_row_tiles = m // ROW_TILE
    ch = _side_cast_rows(f, nf * n_row_tiles)
    n_chunks = f // ch
    chunk = lambda n, i: jnp.minimum(n * n_row_tiles + i, n_chunks - 1)
    return pl.pallas_call(
        _mm_swiglu_kernel,
        grid=(nf, n_row_tiles),
        in_specs=[pl.BlockSpec((ROW_TILE, k), lambda n, i: (i, 0)),
                  pl.BlockSpec((None, k, tn), lambda n, i: (layer, 0, n)),
                  pl.BlockSpec((None, k, tn), lambda n, i: (layer, 0, n + nf)),
                  pl.BlockSpec((None, ch, d_out), lambda n, i: (layer, chunk(n, i), 0))],
        out_specs=[pl.BlockSpec((ROW_TILE, tn), lambda n, i: (i, n)),
                   pl.BlockSpec((ch, d_out), lambda n, i: (chunk(n, i), 0))],
        out_shape=[jax.ShapeDtypeStruct((m, f), BF16), jax.ShapeDtypeStruct((f, d_out), BF16)],
        scratch_shapes=[pltpu.VMEM((k, tn), BF16), pltpu.VMEM((k, tn), BF16)],
        compiler_params=_cparams(2),
        name="mm_swiglu",
    )(a, w13, w13, w2)


def _mm_hgrn_in(a, w, layer):
    m, k = a.shape
    n_total = w.shape[-1]
    tn = COL_TILE
    seg = HGRN_HEADS * HGRN_DK
    assert n_total == 5 * seg and seg % tn == 0
    return pl.pallas_call(
        functools.partial(_mm_hgrn_in_kernel, tiles_per_segment=seg // tn),
        grid=(n_total // tn, m // ROW_TILE),
        in_specs=[pl.BlockSpec((ROW_TILE, k), lambda n, i: (i, 0)),
                  pl.BlockSpec((None, k, tn), lambda n, i: (layer, 0, n))],
        out_specs=pl.BlockSpec((ROW_TILE, tn), lambda n, i: (i, n)),
        out_shape=jax.ShapeDtypeStruct((m, n_total), BF16),
        scratch_shapes=[pltpu.VMEM((k, tn), BF16)],
        compiler_params=_cparams(2),
        name="hgrn_in",
    )(a, w)


def _mm_resid_bf16w(a, w_bf16, h, mods, rows, mod_layer, which, tn, name):
    m, k = a.shape
    n_total = w_bf16.shape[-1]
    return pl.pallas_call(
        _mm_resid_bf16w_kernel,
        grid=(n_total // tn, m // ROW_TILE),
        in_specs=[pl.BlockSpec((ROW_TILE, k), lambda n, i: (i, 0)),
                  pl.BlockSpec((k, tn), lambda n, i: (0, n)),
                  pl.BlockSpec((ROW_TILE, tn), lambda n, i: (i, n)),
                  _mod_spec(rows, mod_layer, which, tn, lambda n: n)],
        out_specs=pl.BlockSpec((ROW_TILE, tn), lambda n, i: (i, n)),
        out_shape=jax.ShapeDtypeStruct((m, n_total), F32),
        compiler_params=_cparams(2),
        name=name,
    )(a, w_bf16, h, mods)


def _mm_resid(a, w, layer, h, mods, rows, mod_layer, which, tn, name="mm_resid"):
    m, k = a.shape
    n_total = w.shape[-1]
    return pl.pallas_call(
        _mm_resid_kernel,
        grid=(n_total // tn, m // ROW_TILE),
        in_specs=[pl.BlockSpec((ROW_TILE, k), lambda n, i: (i, 0)),
                  pl.BlockSpec((None, k, tn), lambda n, i: (layer, 0, n)),
                  pl.BlockSpec((ROW_TILE, tn), lambda n, i: (i, n)),
                  _mod_spec(rows, mod_layer, which, tn, lambda n: n)],
        out_specs=pl.BlockSpec((ROW_TILE, tn), lambda n, i: (i, n)),
        out_shape=jax.ShapeDtypeStruct((m, n_total), F32),
        scratch_shapes=[pltpu.VMEM((k, tn), BF16)],
        compiler_params=_cparams(2),
        name=name,
    )(a, w, h, mods)


def _norm_mod(h, norm_w, mods, rows, layer, which_shift):
    m, d = h.shape
    return pl.pallas_call(
        _norm_mod_kernel,
        grid=(1, m // ROW_TILE),
        in_specs=[pl.BlockSpec((ROW_TILE, d), lambda n, i: (i, 0)),
                  pl.BlockSpec((None, 1, d), lambda n, i: (layer, 0, 0)),
                  _mod_spec(rows, layer, which_shift, d, lambda n: 0),
                  _mod_spec(rows, layer, which_shift + 1, d, lambda n: 0)],
        out_specs=pl.BlockSpec((ROW_TILE, d), lambda n, i: (i, 0)),
        out_shape=jax.ShapeDtypeStruct((m, d), BF16),
        compiler_params=_cparams(2),
        name="norm_mod",
    )(h, norm_w, mods, mods)


def _norm_route(h, norm_w, mods, rows, layer, router_pad, moe_layer):
    m, d = h.shape
    return pl.pallas_call(
        _norm_route_kernel,
        grid=(1, m // ROW_TILE),
        in_specs=[pl.BlockSpec((ROW_TILE, d), lambda n, i: (i, 0)),
                  pl.BlockSpec((None, 1, d), lambda n, i: (layer, 0, 0)),
                  _mod_spec(rows, layer, 3, d, lambda n: 0),
                  _mod_spec(rows, layer, 4, d, lambda n: 0),
                  pl.BlockSpec((None, d, LANES), lambda n, i: (moe_layer, 0, 0))],
        out_specs=[pl.BlockSpec((ROW_TILE, d), lambda n, i: (i, 0)),
                   pl.BlockSpec((ROW_TILE, LANES), lambda n, i: (i, 0))],
        out_shape=[jax.ShapeDtypeStruct((m, d), F32), jax.ShapeDtypeStruct((m, LANES), F32)],
        compiler_params=_cparams(2),
        name="norm_route",
    )(h, norm_w, mods, mods, router_pad)


def _final_norm(h, norm_w, rows):
    d = h.shape[1]
    first = rows.n_ctx // ROW_TILE
    n_lat = rows.batch * rows.seq
    return pl.pallas_call(
        _final_norm_kernel,
        grid=(n_lat // ROW_TILE,),
        in_specs=[pl.BlockSpec((ROW_TILE, d), lambda i: (first + i, 0)),
                  pl.BlockSpec((1, d), lambda i: (0, 0))],
        out_specs=pl.BlockSpec((ROW_TILE, d), lambda i: (i, 0)),
        out_shape=jax.ShapeDtypeStruct((n_lat, d), F32),
        compiler_params=_cparams(1),
        name="final_norm",
    )(h, norm_w.reshape(1, d))


def _rope_tables(rows):
    s = rows.seq
    pos = np.arange(s)
    row = (pos // GRID_W).astype(np.float64)
    col = (pos % GRID_W).astype(np.float64)
    sec = HEAD_DIM // 2
    inv = ROPE_THETA ** (-np.arange(0, sec, 2, dtype=np.float64) / sec)
    inv = inv.astype(np.float32).astype(np.float64)
    ang = np.concatenate([row[:, None] * inv, row[:, None] * inv, col[:, None] * inv, col[:, None] * inv], axis=1)
    ang = ang.astype(np.float32).astype(np.float64)
    cos, sin = np.cos(ang), np.sin(ang)
    first_half = (np.arange(HEAD_DIM) % sec) < (sec // 2)
    sa = np.where(first_half[None, :], -sin, 0.0)
    sb = np.where(first_half[None, :], 0.0, sin)
    ident = np.zeros((rows.n_ctx, HEAD_DIM))
    mk = lambda ctx_rows, lat: jnp.asarray(np.concatenate([ctx_rows, lat], axis=0), dtype=F32)
    return mk(ident + 1.0, cos), mk(ident, sa), mk(ident, sb)


def _attn_block(q_ref, o_ref, sink, s_keys, v_keys, band_scores):
    q = jnp.concatenate([q_ref[:, g * HEAD_DIM:(g + 1) * HEAD_DIM] for g in range(ATTN_GROUP)], axis=0)
    s_c = _dot_nt(q, s_keys[0])
    m = jnp.maximum(jnp.max(s_c, axis=-1, keepdims=True), sink)
    if band_scores is not None:
        s_b = band_scores(_dot_nt(q, s_keys[1]))
        m = jnp.maximum(m, jnp.max(s_b, axis=-1, keepdims=True))
    p_c = jnp.exp(s_c - m)
    den = jnp.sum(p_c, axis=-1, keepdims=True) + jnp.exp(sink - m)
    o = _dot(p_c.astype(BF16), v_keys[0])
    if band_scores is not None:
        p_b = jnp.exp(s_b - m)
        den = den + jnp.sum(p_b, axis=-1, keepdims=True)
        o = o + _dot(p_b.astype(BF16), v_keys[1])
    o = o * (1.0 / den)
    for g in range(ATTN_GROUP):
        o_ref[:, g * HEAD_DIM:(g + 1) * HEAD_DIM] = o[g * ATTN_BLOCK:(g + 1) * ATTN_BLOCK].astype(o_ref.dtype)


def _attn_kernel(sink_ref, q_ref, kc_ref, vc_ref, kp_ref, ko_ref, kn_ref, vp_ref, vo_ref, vn_ref, o_ref, *,
                 slot, n_ctx_blocks, n_blocks):
    hkv = pl.program_id(1)
    step = pl.program_id(2)
    rows_q = ATTN_GROUP * ATTN_BLOCK
    grp = lax.broadcasted_iota(jnp.int32, (rows_q, 1), 0) // ATTN_BLOCK
    sink = jnp.zeros((rows_q, 1), F32)
    for g in range(ATTN_GROUP):
        sink = jnp.where(grp == g, sink_ref[slot, hkv * ATTN_GROUP + g], sink)

    @pl.when(step < n_ctx_blocks)
    def _():
        _attn_block(q_ref, o_ref, sink, (kc_ref[...],), (vc_ref[...],), None)

    @pl.when(step >= n_ctx_blocks)
    def _():
        n = step - n_ctx_blocks
        kb = jnp.concatenate([kp_ref[...], ko_ref[...], kn_ref[...]], axis=0)
        vb = jnp.concatenate([vp_ref[...], vo_ref[...], vn_ref[...]], axis=0)

        def band_scores(s_b):
            i = lax.broadcasted_iota(jnp.int32, s_b.shape, 0) & (ATTN_BLOCK - 1)
            j = lax.broadcasted_iota(jnp.int32, s_b.shape, 1)
            lo = jnp.where(n > 0, 0, ATTN_BLOCK)
            hi = jnp.where(n < n_blocks - 1, 3 * ATTN_BLOCK, 2 * ATTN_BLOCK)
            valid = (j >= i) & (j <= i + 2 * ATTN_BLOCK) & (j >= lo) & (j < hi)
            return jnp.where(valid, s_b, NEG_INF)

        _attn_block(q_ref, o_ref, sink, (kc_ref[...], kb), (vc_ref[...], vb), band_scores)


def _attention(p, sink, slot, rows):
    r = p.shape[0]
    dq = ATTN_HEADS * HEAD_DIM
    gw = ATTN_GROUP * HEAD_DIM
    kcol = dq // HEAD_DIM
    vcol = kcol + ATTN_KV_HEADS
    L, S, B = rows.ctx_len, rows.seq, rows.batch
    nb = S // ATTN_BLOCK
    nbc = L // ATTN_BLOCK
    lat0 = rows.n_ctx // ATTN_BLOCK

    def q_block(b, s):
        return jnp.where(s < nbc, b * nbc + s, lat0 + b * nb + (s - nbc))

    def band(col0, shift):
        return pl.BlockSpec((ATTN_BLOCK, HEAD_DIM),
                            lambda b, h, s: (lat0 + b * nb + jnp.clip(s - nbc + shift, 0, nb - 1), col0 + h))

    return pl.pallas_call(
        functools.partial(_attn_kernel, slot=slot, n_ctx_blocks=nbc, n_blocks=nb),
        grid=(B, ATTN_KV_HEADS, nbc + nb),
        in_specs=[pl.BlockSpec(memory_space=pltpu.SMEM),
                  pl.BlockSpec((ATTN_BLOCK, gw), lambda b, h, s: (q_block(b, s), h)),
                  pl.BlockSpec((L, HEAD_DIM), lambda b, h, s: (b, kcol + h)),
                  pl.BlockSpec((L, HEAD_DIM), lambda b, h, s: (b, vcol + h)),
                  band(kcol, -1), band(kcol, 0), band(kcol, 1), band(vcol, -1), band(vcol, 0), band(vcol, 1)],
        out_specs=pl.BlockSpec((ATTN_BLOCK, gw), lambda b, h, s: (q_block(b, s), h)),
        out_shape=jax.ShapeDtypeStruct((r, dq), BF16),
        compiler_params=_cparams(3),
        name="attention",
    )(sink, p, p, p, p, p, p, p, p, p)


def _dft_cs(n, scale):
    k = np.arange(n)
    ang = 2.0 * np.pi * ((k[:, None] * k[None, :]) % n) / n
    return np.cos(ang) * scale, np.sin(ang) * scale


def _chan_dft_kernel(u_ref, m_ref, o_ref, *, gd):
    for g in range(u_ref.shape[1] // gd):
        v = _dot(u_ref[:, g * gd:(g + 1) * gd], m_ref[...])
        o_ref[0, :, g * gd:(g + 1) * gd] = v[:, :gd].astype(o_ref.dtype)
        o_ref[1, :, g * gd:(g + 1) * gd] = v[:, gd:].astype(o_ref.dtype)


def _chan_dft(u, row0, n_rows, mat):
    d = u.shape[1]
    gd = d // FNET_GROUPS
    t0 = row0 // ROW_TILE
    return pl.pallas_call(
        functools.partial(_chan_dft_kernel, gd=gd),
        grid=(n_rows // ROW_TILE,),
        in_specs=[pl.BlockSpec((ROW_TILE, d), lambda i: (t0 + i, 0)),
                  pl.BlockSpec((gd, 2 * gd), lambda i: (0, 0))],
        out_specs=pl.BlockSpec((2, ROW_TILE, d), lambda i: (0, i, 0)),
        out_shape=jax.ShapeDtypeStruct((2, n_rows, d), BF16),
        compiler_params=_cparams(1),
        name="fnet_chan_dft",
    )(u, mat)


def _seq_dft_a_kernel(x_ref, chan_ref, m_ref, tc_ref, ts_ref, o_ref, *, gd):
    n1 = x_ref.shape[0]
    parts = [_dot(x_ref[:, g * gd:(g + 1) * gd], chan_ref[...]) for g in range(x_ref.shape[1] // gd)]
    vr = jnp.concatenate([p[:, :gd] for p in parts], axis=1).astype(BF16)
    vi = jnp.concatenate([p[:, gd:] for p in parts], axis=1).astype(BF16)
    z = _dot(m_ref[...], jnp.concatenate([vr, vi], axis=0))
    zr, zi = z[:n1], z[n1:]
    tc, ts = tc_ref[...], ts_ref[...]
    for c in range(z.shape[1] // LANES):
        sl = slice(c * LANES, (c + 1) * LANES)
        o_ref[0, :, sl] = (zr[:, sl] * tc + zi[:, sl] * ts).astype(o_ref.dtype)
        o_ref[1, :, sl] = (zi[:, sl] * tc - zr[:, sl] * ts).astype(o_ref.dtype)


def _seq_dft_c_kernel(x_ref, m_ref, o_ref):
    x = jnp.concatenate([x_ref[0], x_ref[1]], axis=0)
    o_ref[...] = _dot(m_ref[...], x).astype(o_ref.dtype)


def _fourier_tokens(u, rows):
    d = u.shape[1]
    gd = d // FNET_GROUPS
    B, S, L = rows.batch, rows.seq, rows.ctx_len
    n2 = GRID_W
    n1 = S // n2
    cc, sc = _dft_cs(gd, gd ** -0.5)
    chan = jnp.asarray(np.concatenate([cc, -sc], axis=1), dtype=BF16)

    x_lat = u[rows.n_ctx:].reshape(B, n1, n2 * d)
    c1, s1 = _dft_cs(n1, n1 ** -0.5)
    m1 = jnp.asarray(np.block([[c1, s1], [-s1, c1]]), dtype=BF16)
    ang = 2.0 * np.pi * (np.arange(n2)[:, None] * np.arange(n1)[None, :]) / S
    tw_c = jnp.asarray(np.broadcast_to(np.cos(ang)[:, :, None], (n2, n1, LANES)), dtype=F32)
    tw_s = jnp.asarray(np.broadcast_to(np.sin(ang)[:, :, None], (n2, n1, LANES)), dtype=F32)
    z = pl.pallas_call(
        functools.partial(_seq_dft_a_kernel, gd=gd),
        grid=(B, n2),
        in_specs=[pl.BlockSpec((None, n1, d), lambda b, t: (b, 0, t)),
                  pl.BlockSpec((gd, 2 * gd), lambda b, t: (0, 0)),
                  pl.BlockSpec((2 * n1, 2 * n1), lambda b, t: (0, 0)),
                  pl.BlockSpec((None, n1, LANES), lambda b, t: (t, 0, 0)),
                  pl.BlockSpec((None, n1, LANES), lambda b, t: (t, 0, 0))],
        out_specs=pl.BlockSpec((2, None, None, n1, d), lambda b, t: (0, b, t, 0, 0)),
        out_shape=jax.ShapeDtypeStruct((2, B, n2, n1, d), BF16),
        compiler_params=_cparams(2),
        name="fnet_seq_dft_a",
    )(x_lat, chan, m1, tw_c, tw_s)
    z = z.reshape(2, B, n2, n1 * d)
    c2, s2 = _dft_cs(n2, n2 ** -0.5)
    m2 = jnp.asarray(np.concatenate([c2, s2], axis=1), dtype=BF16)
    tc = 8192
    assert (n1 * d) % tc == 0
    y_lat = pl.pallas_call(
        _seq_dft_c_kernel,
        grid=(B, n1 * d // tc),
        in_specs=[pl.BlockSpec((2, None, n2, tc), lambda b, j: (0, b, 0, j)),
                  pl.BlockSpec((n2, 2 * n2), lambda b, j: (0, 0))],
        out_specs=pl.BlockSpec((None, n2, tc), lambda b, j: (b, 0, j)),
        out_shape=jax.ShapeDtypeStruct((B, n2, n1 * d), BF16),
        compiler_params=_cparams(2),
        name="fnet_seq_dft_c",
    )(z, m2)

    vc = _chan_dft(u, 0, B * L, chan).reshape(2, B, L, d)
    cl, sl = _dft_cs(L, L ** -0.5)
    ml = jnp.asarray(np.concatenate([cl, sl], axis=1), dtype=BF16)
    y_ctx = pl.pallas_call(
        _seq_dft_c_kernel,
        grid=(B, 1),
        in_specs=[pl.BlockSpec((2, None, L, d), lambda b, j: (0, b, 0, 0)),
                  pl.BlockSpec((L, 2 * L), lambda b, j: (0, 0))],
        out_specs=pl.BlockSpec((None, L, d), lambda b, j: (b, 0, 0)),
        out_shape=jax.ShapeDtypeStruct((B, L, d), BF16),
        compiler_params=_cparams(2),
        name="fnet_ctx_dft",
    )(vc, ml)
    return jnp.concatenate([y_ctx.reshape(B * L, d), y_lat.reshape(B * S, d)], axis=0)


def _hgrn_scan_kernel(q_ref, f_ref, v_ref, lb_ref, tri_ref, o_ref, st_ref, *, reverse):
    @pl.when(pl.program_id(1) == 0)
    def _():
        st_ref[...] = jnp.zeros(st_ref.shape, st_ref.dtype)

    c = HGRN_CHUNK
    n_sub = q_ref.shape[0] // c
    lb = lb_ref[...]
    tri = tri_ref[...]
    ti = lax.broadcasted_iota(jnp.int32, (c, c), 0)
    si = lax.broadcasted_iota(jnp.int32, (c, c), 1)
    keep = (si >= ti) if reverse else (si <= ti)
    order = range(n_sub - 1, -1, -1) if reverse else range(n_sub)
    for sub in order:
        rs = slice(sub * c, (sub + 1) * c)
        q = q_ref[rs, :].astype(F32)
        f = lb + (1.0 - lb) * _sigmoid(f_ref[rs, :].astype(F32))
        k = 1.0 - f
        bsum = jnp.dot(tri, jnp.log(f), preferred_element_type=F32, precision=lax.Precision.HIGHEST)
        b_end = bsum[0:1, :] if reverse else bsum[c - 1:c, :]
        decay = jnp.exp(b_end)
        q_in = (q * jnp.exp(bsum)).astype(BF16)
        k_inf = k * jnp.exp(-bsum)
        k_in = k_inf.astype(BF16)
        k_out = (k_inf * decay).astype(BF16)
        v = v_ref[rs, :]
        vt = v.astype(F32)
        for h in range(HGRN_HEADS):
            ks = slice(h * HGRN_DK, (h + 1) * HGRN_DK)
            vs = slice(h * HGRN_DV, (h + 1) * HGRN_DV)
            a = jnp.where(keep, _dot_nt(q_in[:, ks], k_in[:, ks]), 0.0)
            st = st_ref[h]
            o = _dot(a.astype(BF16), v[:, vs]) + _dot_nt(q_in[:, ks], st.astype(BF16))
            o_ref[rs, vs] = o
            st_ref[h] = st * decay[:, ks] + _dot(vt[:, vs].T.astype(BF16), k_out[:, ks])


def _hgrn_scan(pm, lb, rows, reverse):
    r = pm.shape[0]
    hk = HGRN_HEADS * HGRN_DK
    B, S, L = rows.batch, rows.seq, rows.ctx_len
    cs, ls = L // SCAN_ROWS, S // SCAN_ROWS
    direction = 1 if reverse else 0

    def row_block(b, s):
        if reverse:
            ctx = b * cs + (cs - 1 - s)
            lat = B * cs + b * ls + (ls - 1 - (s - cs))
        else:
            ctx = b * cs + s
            lat = B * cs + b * ls + (s - cs)
        return jnp.where(s < cs, ctx, lat)

    c = HGRN_CHUNK
    tri_np = np.triu(np.ones((c, c))) if reverse else np.tril(np.ones((c, c)))
    tri = jnp.asarray(tri_np, dtype=F32)
    col = lambda j: pl.BlockSpec((SCAN_ROWS, hk), lambda b, s: (row_block(b, s), j))
    return pl.pallas_call(
        functools.partial(_hgrn_scan_kernel, reverse=reverse),
        grid=(B, cs + ls),
        in_specs=[col(0), col(1 + direction), col(3),
                  pl.BlockSpec((None, 1, hk), lambda b, s: (direction, 0, 0)),
                  pl.BlockSpec((c, c), lambda b, s: (0, 0))],
        out_specs=pl.BlockSpec((SCAN_ROWS, HGRN_HEADS * HGRN_DV), lambda b, s: (row_block(b, s), 0)),
        out_shape=jax.ShapeDtypeStruct((r, HGRN_HEADS * HGRN_DV), F32),
        scratch_shapes=[pltpu.VMEM((HGRN_HEADS, HGRN_DV, HGRN_DK), F32)],
        compiler_params=_cparams(2),
        name="hgrn_scan_bwd" if reverse else "hgrn_scan_fwd",
    )(pm, pm, pm, lb, tri)


def _hgrn_readout_kernel(of_ref, ob_ref, g_ref, ng_ref, o_ref):
    for h in range(HGRN_HEADS):
        vs = slice(h * HGRN_DV, (h + 1) * HGRN_DV)
        o = of_ref[:, vs] + ob_ref[:, vs]
        o = o * lax.rsqrt(jnp.mean(o * o, axis=-1, keepdims=True) + RMS_EPS) * ng_ref[:, vs]
        o_ref[:, vs] = (o * g_ref[:, vs].astype(F32)).astype(o_ref.dtype)


def _hgrn_readout(o_f, o_b, pm, norm_g, slot):
    r, d = o_f.shape
    gcol = pm.shape[1] // d - 1
    return pl.pallas_call(
        _hgrn_readout_kernel,
        grid=(r // ROW_TILE,),
        in_specs=[pl.BlockSpec((ROW_TILE, d), lambda i: (i, 0)),
                  pl.BlockSpec((ROW_TILE, d), lambda i: (i, 0)),
                  pl.BlockSpec((ROW_TILE, d), lambda i: (i, gcol)),
                  pl.BlockSpec((None, 1, d), lambda i: (slot, 0, 0))],
        out_specs=pl.BlockSpec((ROW_TILE, d), lambda i: (i, 0)),
        out_shape=jax.ShapeDtypeStruct((r, d), BF16),
        compiler_params=_cparams(1),
        name="hgrn_readout",
    )(o_f, o_b, pm, norm_g)


def _moe_plan(route, n_rows, n_tiles):
    e_flat = jnp.concatenate([route[:, 0], route[:, 1]]).astype(jnp.int32)
    t_flat = jnp.tile(jnp.arange(n_rows, dtype=jnp.int32), 2)
    onehot = (e_flat[:, None] == jnp.arange(N_EXPERTS, dtype=jnp.int32)[None, :]).astype(jnp.int32)
    csum = jnp.cumsum(onehot, axis=0)
    counts = csum[-1]
    rank = jnp.sum(csum * onehot, axis=1) - 1
    padded = ((counts + MOE_ROW_TILE - 1) // MOE_ROW_TILE) * MOE_ROW_TILE
    ends = jnp.cumsum(padded)
    starts = ends - padded
    dest = jnp.sum(starts[None, :] * onehot, axis=1) + rank
    p_rows = n_tiles * MOE_ROW_TILE
    row_token = jnp.zeros((p_rows,), jnp.int32).at[dest].set(t_flat)
    n_used = (ends[-1] // MOE_ROW_TILE).astype(jnp.int32)
    tile = jnp.arange(n_tiles, dtype=jnp.int32)
    te = jnp.sum((ends[None, :] <= (tile * MOE_ROW_TILE)[:, None]).astype(jnp.int32), axis=1)
    te = jnp.minimum(te, N_EXPERTS - 1)
    te_last = jnp.sum(jnp.where(tile == n_used - 1, te, 0))
    te = jnp.where(tile < n_used, te, te_last).astype(jnp.int32)
    return row_token, dest, te, n_used.reshape(1)


def _moe_gather_kernel(tok_ref, nu_ref, src_ref, o_ref, buf_ref, sem):
    i = pl.program_id(0)
    n_used = nu_ref[0]

    def row_copy(slot, r, t):
        return pltpu.make_async_copy(src_ref.at[pl.ds(t, 1), :], buf_ref.at[slot, pl.ds(r, 1), :], sem.at[slot])

    def issue_tile(tile):
        slot = tile % 2
        base = tile * MOE_ROW_TILE

        def issue(r, carry):
            row_copy(slot, r, tok_ref[base + r]).start()
            return carry

        lax.fori_loop(0, MOE_ROW_TILE, issue, 0, unroll=8)

    @pl.when(i == 0)
    def _():
        issue_tile(i)

    @pl.when(i + 1 < n_used)
    def _():
        issue_tile(i + 1)

    @pl.when(i < n_used)
    def _():
        slot = i % 2

        def drain(r, carry):
            row_copy(slot, r, 0).wait()
            return carry

        lax.fori_loop(0, MOE_ROW_TILE, drain, 0, unroll=8)
        o_ref[...] = buf_ref[slot].astype(o_ref.dtype)

    @pl.when(i >= n_used)
    def _():
        o_ref[...] = jnp.zeros(o_ref.shape, o_ref.dtype)


def _moe_gather(row_token, n_used, u, n_tiles):
    d = u.shape[1]
    return pl.pallas_call(
        _moe_gather_kernel,
        grid_spec=pltpu.PrefetchScalarGridSpec(
            num_scalar_prefetch=2,
            grid=(n_tiles,),
            in_specs=[pl.BlockSpec(memory_space=pl.ANY)],
            out_specs=pl.BlockSpec((MOE_ROW_TILE, d), lambda i, tok, nu: (i, 0)),
            scratch_shapes=[pltpu.VMEM((2, MOE_ROW_TILE, d), F32), pltpu.SemaphoreType.DMA((2,))]),
        out_shape=jax.ShapeDtypeStruct((n_tiles * MOE_ROW_TILE, d), BF16),
        compiler_params=_cparams(1),
        name="moe_gather",
    )(row_token, n_used, u)


def _expert_changed(te_ref):
    i = pl.program_id(1)
    return (i == 0) | (te_ref[i] != te_ref[jnp.maximum(i - 1, 0)])


def _gmm_swiglu_kernel(te_ref, nu_ref, x_ref, wa_ref, wb_ref, w2_ref, o_ref, w2bf_ref, wbfa_ref, wbfb_ref):
    i = pl.program_id(1)

    @pl.when(_expert_changed(te_ref))
    def _():
        wbfa_ref[...] = wa_ref[...].astype(BF16)
        wbfb_ref[...] = wb_ref[...].astype(BF16)

    @pl.when(i < nu_ref[0])
    def _():
        x = x_ref[...]
        o_ref[...] = (_silu(_dot(x, wbfa_ref[...])) * _dot(x, wbfb_ref[...])).astype(o_ref.dtype)

    @pl.when(i >= nu_ref[0])
    def _():
        o_ref[...] = jnp.zeros(o_ref.shape, o_ref.dtype)

    w2bf_ref[...] = w2_ref[...].astype(BF16)


def _gmm_down_kernel(te_ref, nu_ref, x_ref, w_ref, o_ref):
    i = pl.program_id(1)

    @pl.when(i < nu_ref[0])
    def _():
        o_ref[...] = _dot(x_ref[...], w_ref[...])

    @pl.when(i >= nu_ref[0])
    def _():
        o_ref[...] = jnp.zeros(o_ref.shape, o_ref.dtype)


def _moe_experts(xg, te, n_used, w13, w2, layer, n_tiles):
    p_rows, d = xg.shape
    n_layers, n_exp, f, d_out = w2.shape
    tn = COL_TILE
    nf = f // tn
    used = lambda i, nu: jnp.minimum(i, nu[0] - 1)
    ch = _side_cast_rows(n_exp * f, nf * n_tiles)
    n_chunks = n_exp * f // ch
    chunk = lambda n, i: jnp.minimum(n * n_tiles + i, n_chunks - 1)
    hidden, w2_bf16 = pl.pallas_call(
        _gmm_swiglu_kernel,
        grid_spec=pltpu.PrefetchScalarGridSpec(
            num_scalar_prefetch=2,
            grid=(nf, n_tiles),
            in_specs=[pl.BlockSpec((MOE_ROW_TILE, d), lambda n, i, te, nu: (used(i, nu), 0)),
                      pl.BlockSpec((None, None, d, tn), lambda n, i, te, nu: (layer, te[i], 0, n)),
                      pl.BlockSpec((None, None, d, tn), lambda n, i, te, nu: (layer, te[i], 0, n + nf)),
                      pl.BlockSpec((None, ch, d_out), lambda n, i, te, nu: (layer, chunk(n, i), 0))],
            out_specs=[pl.BlockSpec((MOE_ROW_TILE, tn), lambda n, i, te, nu: (i, n)),
                       pl.BlockSpec((ch, d_out), lambda n, i, te, nu: (chunk(n, i), 0))],
            scratch_shapes=[pltpu.VMEM((d, tn), BF16), pltpu.VMEM((d, tn), BF16)]),
        out_shape=[jax.ShapeDtypeStruct((p_rows, f), BF16), jax.ShapeDtypeStruct((n_exp * f, d_out), BF16)],
        compiler_params=_cparams(2),
        name="moe_w13",
    )(te, n_used, xg, w13, w13, w2.reshape(n_layers, n_exp * f, d_out))
    tn2 = MOE_W2_COL_TILE
    return pl.pallas_call(
        _gmm_down_kernel,
        grid_spec=pltpu.PrefetchScalarGridSpec(
            num_scalar_prefetch=2,
            grid=(d_out // tn2, n_tiles),
            in_specs=[pl.BlockSpec((MOE_ROW_TILE, f), lambda n, i, te, nu: (used(i, nu), 0)),
                      pl.BlockSpec((None, f, tn2), lambda n, i, te, nu: (te[i], 0, n))],
            out_specs=pl.BlockSpec((MOE_ROW_TILE, tn2), lambda n, i, te, nu: (i, n)),
            scratch_shapes=[]),
        out_shape=jax.ShapeDtypeStruct((p_rows, d_out), F32),
        compiler_params=_cparams(2),
        name="moe_w2",
    )(te, n_used, hidden, w2_bf16.reshape(n_exp, f, d_out))


def _moe_combine_kernel(p1_ref, p2_ref, y_ref, h_ref, g_ref, route_ref, o_ref, buf_ref, sem):
    base = pl.program_id(0) * COMBINE_ROW_TILE

    def row_copy(k, r, p):
        return pltpu.make_async_copy(y_ref.at[pl.ds(p, 1), :], buf_ref.at[k, pl.ds(r, 1), :], sem)

    def issue(r, carry):
        row_copy(0, r, p1_ref[base + r]).start()
        row_copy(1, r, p2_ref[base + r]).start()
        return carry

    def drain(r, carry):
        row_copy(0, r, 0).wait()
        row_copy(1, r, 0).wait()
        return carry

    lax.fori_loop(0, COMBINE_ROW_TILE, issue, 0, unroll=8)
    lax.fori_loop(0, COMBINE_ROW_TILE, drain, 0, unroll=8)
    w1 = route_ref[:, 2:3]
    w2 = route_ref[:, 3:4]
    o_ref[...] = h_ref[...] + g_ref[...] * (w1 * buf_ref[0] + w2 * buf_ref[1])


def _moe_combine(pos1, pos2, y, h, mods, route, rows, mod_layer):
    r, d = h.shape
    g = rows.groups
    tr = COMBINE_ROW_TILE
    return pl.pallas_call(
        _moe_combine_kernel,
        grid_spec=pltpu.PrefetchScalarGridSpec(
            num_scalar_prefetch=2,
            grid=(r // tr,),
            in_specs=[pl.BlockSpec(memory_space=pl.ANY),
                      pl.BlockSpec((tr, d), lambda i, p1, p2: (i, 0)),
                      pl.BlockSpec((None, 1, d),
                                   lambda i, p1, p2: ((mod_layer * 6 + 5) * g + rows.group(i, tr), 0, 0)),
                      pl.BlockSpec((tr, LANES), lambda i, p1, p2: (i, 0))],
            out_specs=pl.BlockSpec((tr, d), lambda i, p1, p2: (i, 0)),
            scratch_shapes=[pltpu.VMEM((2, tr, d), F32), pltpu.SemaphoreType.DMA(())]),
        out_shape=jax.ShapeDtypeStruct((r, d), F32),
        compiler_params=_cparams(1),
        name="moe_combine",
    )(pos1, pos2, y, h, mods, route)


def _moe_ffn(h, norm_w, mods, rows, layer, router_pad, w13, w2, moe_layer):
    r = h.shape[0]
    n_tiles = -(-(2 * r + N_EXPERTS * (MOE_ROW_TILE - 1)) // MOE_ROW_TILE)
    u, route = _norm_route(h, norm_w, mods, rows, layer, router_pad, moe_layer)
    row_token, dest, te, n_used = _moe_plan(route, r, n_tiles)
    xg = _moe_gather(row_token, n_used, u, n_tiles)
    y = _moe_experts(xg, te, n_used, w13, w2, moe_layer, n_tiles)
    return _moe_combine(dest[:r], dest[r:], y, h, mods, route, rows, layer)


def _hgrn_lower_bounds(lb_logits, layer):
    gamma = jax.nn.softmax(lb_logits.astype(F32), axis=0)
    lb = jnp.cumsum(gamma, axis=0) - gamma[0]
    return lb[layer]


def kernel(x, c, ctx, c_ctx, ada_w, ada_b, norm_mix, norm_ffn, norm_final, attn_wqkv, attn_wo, attn_sink, fnet_wo,
           hgrn_win, hgrn_lb, hgrn_norm, hgrn_wo, ffn_w13, ffn_w2, moe_router, moe_w13, moe_w2):
    B, S, D = x.shape
    L = ctx.shape[1]
    depth = ada_w.shape[0]
    rows = _Rows(B, S, L)
    G = rows.groups

    cond = jnp.concatenate([c_ctx[None, :], c], axis=0)
    cond = jnp.pad(_silu(cond), ((0, 16 - G), (0, 0))).astype(BF16)
    mods = _ada_mods(cond, ada_w, ada_b)
    mods = mods[:, :G, :].reshape(depth, G, 6, D).transpose(0, 2, 1, 3).reshape(depth * 6 * G, 1, D)

    h = jnp.concatenate([ctx.reshape(B * L, D), x.reshape(B * S, D)], axis=0)
    norm_mix3 = norm_mix.reshape(depth, 1, D)
    norm_ffn3 = norm_ffn.reshape(depth, 1, D)
    rope_tables = _rope_tables(rows)
    router_pad = jnp.pad(moe_router, ((0, 0), (0, 0), (0, LANES - moe_router.shape[-1])))

    for i in range(depth):
        kind, slot = i % N_MIXERS, i // N_MIXERS
        u = _norm_mod(h, norm_mix3, mods, rows, i, 0)
        if kind == 0:
            p = _mm_qkv_rope(u, attn_wqkv, slot, rope_tables, rows)
            o = _attention(p, attn_sink, slot, rows)
            h = _mm_resid(o, attn_wo, slot, h, mods, rows, i, 2, COL_TILE, name="attn_out")
        elif kind == 1:
            y = _fourier_tokens(u, rows)
            h = _mm_resid(y, fnet_wo, slot, h, mods, rows, i, 2, COL_TILE, name="fnet_out")
        else:
            lb = _hgrn_lower_bounds(hgrn_lb, i).reshape(2, 1, HGRN_HEADS * HGRN_DK)
            pm = _mm_hgrn_in(u, hgrn_win, slot)
            o_f = _hgrn_scan(pm, lb, rows, reverse=False)
            o_b = _hgrn_scan(pm, lb, rows, reverse=True)
            o = _hgrn_readout(o_f, o_b, pm, hgrn_norm.reshape(-1, 1, D), slot)
            h = _mm_resid(o, hgrn_wo, slot, h, mods, rows, i, 2, COL_TILE, name="hgrn_out")

        j = i // 2
        if i % 2 == 0:
            v = _norm_mod(h, norm_ffn3, mods, rows, i, 3)
            g, w2_bf16 = _mm_swiglu(v, ffn_w13, ffn_w2, j)
            h = _mm_resid_bf16w(g, w2_bf16, h, mods, rows, i, 5, FFN_W2_COL_TILE, name="ffn_out")
        else:
            h = _moe_ffn(h, norm_ffn3, mods, rows, i, router_pad, moe_w13, moe_w2, j)

    return _final_norm(h, norm_final, rows).reshape(B, S, D)
```

```python
import functools

import numpy as np
import jax
import jax.numpy as jnp
from jax import lax
from jax.experimental import pallas as pl
from jax.experimental.pallas import tpu as pltpu

F32 = jnp.float32
BF16 = jnp.bfloat16

N_MIXERS = 3
RMS_EPS = 1e-6
NEG_INF = -1e30
LOG2_E = 1.4426950408889634
GRID_W = 64
ATTN_HEADS = 16
ATTN_KV_HEADS = 4
ATTN_GROUP = ATTN_HEADS // ATTN_KV_HEADS
HEAD_DIM = 128
ATTN_BLOCK = 128
ROPE_THETA = 10000.0
FNET_GROUPS = 8
HGRN_HEADS = 16
HGRN_DK = 128
HGRN_DV = 128
HGRN_CHUNK = 64
N_EXPERTS = 8

LANES = 128
ROW_TILE = 512
COL_TILE = 1024
SWIGLU_COL_TILE = 512
FFN_W2_COL_TILE = 1024
MOE_W2_COL_TILE = 1024
MOE_ROW_TILE = 512
COMBINE_ROW_TILE = 256
SCAN_ROWS = 128
VMEM_LIMIT_BYTES = 56 * 1024 * 1024


def _cparams(n_axes):
    return pltpu.CompilerParams(dimension_semantics=("arbitrary",) * n_axes,
                                vmem_limit_bytes=VMEM_LIMIT_BYTES)


def _dot(a, b):
    return jnp.dot(a, b, preferred_element_type=F32)


def _dot_nt(a, b):
    return lax.dot_general(a, b, (((1,), (1,)), ((), ())), preferred_element_type=F32)


def _sigmoid(x):
    return 1.0 / (1.0 + jnp.exp(-x))


def _silu(x):
    return x * _sigmoid(x)


def _ada_kernel(a_ref, w_ref, b_ref, o_ref):
    o_ref[...] = _dot(a_ref[...], w_ref[...].astype(BF16)) + b_ref[...]


def _ada_mods(cond_rows, ada_w, ada_b):
    depth, d, n6 = ada_w.shape
    rows = cond_rows.shape[0]
    tn = 1024
    return pl.pallas_call(
        _ada_kernel,
        grid=(depth, n6 // tn),
        in_specs=[pl.BlockSpec((rows, d), lambda l, n: (0, 0)),
                  pl.BlockSpec((None, d, tn), lambda l, n: (l, 0, n)),
                  pl.BlockSpec((None, 1, tn), lambda l, n: (l, 0, n))],
        out_specs=pl.BlockSpec((None, rows, tn), lambda l, n: (l, 0, n)),
        out_shape=jax.ShapeDtypeStruct((depth, rows, n6), F32),
        compiler_params=_cparams(2),
        name="ada_mods",
    )(cond_rows, ada_w, ada_b.reshape(depth, 1, n6))


def _rms_norm(h, gain):
    return h * lax.rsqrt(jnp.mean(h * h, axis=-1, keepdims=True) + RMS_EPS) * gain


def _norm_mod_kernel(h_ref, g_ref, sh_ref, sc_ref, o_ref):
    o_ref[...] = (_rms_norm(h_ref[...], g_ref[...]) * (1.0 + sc_ref[...]) + sh_ref[...]).astype(o_ref.dtype)


def _top2_route(u, router):
    logits = jnp.dot(u, router, preferred_element_type=F32, precision=lax.Precision.HIGHEST)
    lane = lax.broadcasted_iota(jnp.int32, logits.shape, 1)
    l1 = jnp.where(lane < N_EXPERTS, logits, NEG_INF)
    m1 = jnp.max(l1, axis=-1, keepdims=True)
    i1 = jnp.min(jnp.where(l1 == m1, lane, LANES), axis=-1, keepdims=True)
    l2 = jnp.where(lane == i1, NEG_INF, l1)
    m2 = jnp.max(l2, axis=-1, keepdims=True)
    i2 = jnp.min(jnp.where(l2 == m2, lane, LANES), axis=-1, keepdims=True)
    e2 = jnp.exp(m2 - m1)
    w1 = 1.0 / (1.0 + e2)
    w2 = e2 * w1
    return jnp.where(lane == 0, i1.astype(F32),
                     jnp.where(lane == 1, i2.astype(F32), jnp.where(lane == 2, w1, jnp.where(lane == 3, w2, 0.0))))


def _final_norm_kernel(h_ref, g_ref, o_ref):
    o_ref[...] = _rms_norm(h_ref[...], g_ref[...])


def _cast_weight(w_ref, wbf_ref):
    @pl.when(pl.program_id(1) == 0)
    def _():
        wbf_ref[...] = w_ref[...].astype(BF16)


def _mm_rope_kernel(a_ref, w_ref, cos_ref, sa_ref, sb_ref, o_ref, wbf_ref, *, n_q_heads, n_rope_heads, q_scale):
    _cast_weight(w_ref, wbf_ref)
    acc = _dot(a_ref[...], wbf_ref[...])
    heads_per_tile = acc.shape[1] // HEAD_DIM
    first = pl.program_id(0) * heads_per_tile

    @pl.when(first >= n_rope_heads)
    def _():
        o_ref[...] = acc.astype(o_ref.dtype)

    @pl.when(first < n_rope_heads)
    def _():
        cos, sa, sb = cos_ref[...], sa_ref[...], sb_ref[...]
        for c in range(heads_per_tile):
            t = acc[:, c * HEAD_DIM:(c + 1) * HEAD_DIM]
            r = t * cos + pltpu.roll(t, HEAD_DIM - 32, 1) * sa + pltpu.roll(t, 32, 1) * sb
            r = r * jnp.where(first + c < n_q_heads, q_scale, 1.0).astype(F32)
            o_ref[:, c * HEAD_DIM:(c + 1) * HEAD_DIM] = jnp.where(first + c < n_rope_heads, r, t).astype(o_ref.dtype)


def _mm_swiglu_kernel(a_ref, wa_ref, wb_ref, w2_ref, o_ref, w2bf_ref, wbfa_ref, wbfb_ref):
    _cast_weight(wa_ref, wbfa_ref)
    _cast_weight(wb_ref, wbfb_ref)
    a = a_ref[...]
    ga = _dot(a, wbfa_ref[...])
    gb = _dot(a, wbfb_ref[...])
    o_ref[...] = (_silu(ga) * gb).astype(o_ref.dtype)
    w2bf_ref[...] = w2_ref[...].astype(BF16)


def _mm_hgrn_in_kernel(a_ref, w_ref, o_ref, wbf_ref, *, tiles_per_segment):
    _cast_weight(w_ref, wbf_ref)
    seg = pl.program_id(0) // tiles_per_segment
    acc = _dot(a_ref[...], wbf_ref[...])

    @pl.when(seg == 0)
    def _():
        o_ref[...] = _silu(acc).astype(o_ref.dtype)

    @pl.when(seg == 4)
    def _():
        o_ref[...] = _sigmoid(acc).astype(o_ref.dtype)

    @pl.when((seg != 0) & (seg != 4))
    def _():
        o_ref[...] = acc.astype(o_ref.dtype)


def _mm_resid_bf16w_kernel(a_ref, w_ref, h_ref, g_ref, o_ref):
    o_ref[...] = h_ref[...] + g_ref[...] * _dot(a_ref[...], w_ref[...])


def _side_cast_rows(total_rows, n_steps):
    for rows in range(16, total_rows + 1, 16):
        if total_rows % rows == 0 and total_rows // rows <= n_steps:
            return rows
    raise ValueError((total_rows, n_steps))


class _Rows:
    def __init__(self, batch, seq, ctx_len):
        self.batch, self.seq, self.ctx_len = batch, seq, ctx_len
        self.n_ctx = batch * ctx_len
        self.n_rows = self.n_ctx + batch * seq
        self.groups = 1 + batch
        assert self.n_ctx % ROW_TILE == 0 and seq % ROW_TILE == 0, (batch, seq, ctx_len)
        assert ctx_len % SCAN_ROWS == 0 and seq % SCAN_ROWS == 0 and ctx_len % ATTN_BLOCK == 0

    def group(self, tile, tile_rows=ROW_TILE):
        ctx_tiles = self.n_ctx // tile_rows
        per_batch = self.seq // tile_rows
        return jnp.where(tile < ctx_tiles, 0, 1 + (tile - ctx_tiles) // per_batch)


def _mod_spec(rows, layer, which, width, col_of, tile_rows=ROW_TILE, tile0=0):
    g = rows.groups
    return pl.BlockSpec((None, 1, width),
                        lambda n, i: ((layer * 6 + which) * g + rows.group(tile0 + i, tile_rows), 0, col_of(n)))


def _mm_qkv_rope(a, w, layer, tables, rows):
    m, k = a.shape
    n_total = w.shape[-1]
    tn = COL_TILE
    ctx_tiles = rows.n_ctx // ROW_TILE
    seq_tiles = rows.seq // ROW_TILE
    tab = lambda n, i: (jnp.where(i < ctx_tiles, i, ctx_tiles + (i - ctx_tiles) % seq_tiles), 0)
    kern = functools.partial(_mm_rope_kernel, n_q_heads=ATTN_HEADS, n_rope_heads=ATTN_HEADS + ATTN_KV_HEADS,
                             q_scale=LOG2_E * HEAD_DIM ** -0.5)
    return pl.pallas_call(
        kern,
        grid=(n_total // tn, m // ROW_TILE),
        in_specs=[pl.BlockSpec((ROW_TILE, k), lambda n, i: (i, 0)),
                  pl.BlockSpec((None, k, tn), lambda n, i: (layer, 0, n)),
                  pl.BlockSpec((ROW_TILE, HEAD_DIM), tab),
                  pl.BlockSpec((ROW_TILE, HEAD_DIM), tab),
                  pl.BlockSpec((ROW_TILE, HEAD_DIM), tab)],
        out_specs=pl.BlockSpec((ROW_TILE, tn), lambda n, i: (i, n)),
        out_shape=jax.ShapeDtypeStruct((m, n_total), BF16),
        scratch_shapes=[pltpu.VMEM((k, tn), BF16)],
        compiler_params=_cparams(2),
        name="mm_qkv_rope",
    )(a, w, *tables)


def _mm_swiglu(a, w13, w2, layer):
    m, k = a.shape
    f = w13.shape[-1] // 2
    d_out = w2.shape[-1]
    tn = SWIGLU_COL_TILE
    nf = f // tn
    n_row_tiles = m // ROW_TILE
    ch = _side_cast_rows(f, nf * n_row_tiles)
    n_chunks = f // ch
    chunk = lambda n, i: jnp.minimum(n * n_row_tiles + i, n_chunks - 1)
    return pl.pallas_call(
        _mm_swiglu_kernel,
        grid=(nf, n_row_tiles),
        in_specs=[pl.BlockSpec((ROW_TILE, k), lambda n, i: (i, 0)),
                  pl.BlockSpec((None, k, tn), lambda n, i: (layer, 0, n)),
                  pl.BlockSpec((None, k, tn), lambda n, i: (layer, 0, n + nf)),
                  pl.BlockSpec((None, ch, d_out), lambda n, i: (layer, chunk(n, i), 0))],
        out_specs=[pl.BlockSpec((ROW_TILE, tn), lambda n, i: (i, n)),
                   pl.BlockSpec((ch, d_out), lambda n, i: (chunk(n, i), 0))],
        out_shape=[jax.ShapeDtypeStruct((m, f), BF16), jax.ShapeDtypeStruct((f, d_out), BF16)],
        scratch_shapes=[pltpu.VMEM((k, tn), BF16), pltpu.VMEM((k, tn), BF16)],
        compiler_params=_cparams(2),
        name="mm_swiglu",
    )(a, w13, w13, w2)


def _mm_hgrn_in(a, w, layer):
    m, k = a.shape
    n_total = w.shape[-1]
    tn = COL_TILE
    seg = HGRN_HEADS * HGRN_DK
    assert n_total == 5 * seg and seg % tn == 0
    return pl.pallas_call(
        functools.partial(_mm_hgrn_in_kernel, tiles_per_segment=seg // tn),
        grid=(n_total // tn, m // ROW_TILE),
        in_specs=[pl.BlockSpec((ROW_TILE, k), lambda n, i: (i, 0)),
                  pl.BlockSpec((None, k, tn), lambda n, i: (layer, 0, n))],
        out_specs=pl.BlockSpec((ROW_TILE, tn), lambda n, i: (i, n)),
        out_shape=jax.ShapeDtypeStruct((m, n_total), BF16),
        scratch_shapes=[pltpu.VMEM((k, tn), BF16)],
        compiler_params=_cparams(2),
        name="hgrn_in",
    )(a, w)


def _mm_resid_bf16w(a, w_bf16, h, mods, rows, mod_layer, which, tn, name):
    m, k = a.shape
    n_total = w_bf16.shape[-1]
    return pl.pallas_call(
        _mm_resid_bf16w_kernel,
        grid=(n_total // tn, m // ROW_TILE),
        in_specs=[pl.BlockSpec((ROW_TILE, k), lambda n, i: (i, 0)),
                  pl.BlockSpec((k, tn), lambda n, i: (0, n)),
                  pl.BlockSpec((ROW_TILE, tn), lambda n, i: (i, n)),
                  _mod_spec(rows, mod_layer, which, tn, lambda n: n)],
        out_specs=pl.BlockSpec((ROW_TILE, tn), lambda n, i: (i, n)),
        out_shape=jax.ShapeDtypeStruct((m, n_total), F32),
        compiler_params=_cparams(2),
        name=name,
    )(a, w_bf16, h, mods)


def _cast_bf16_kernel(w_ref, o_ref):
    o_ref[...] = w_ref[...].astype(BF16)


def _cast_bf16(w, layer):
    _, k, n = w.shape
    rows_per_step = 256
    return pl.pallas_call(
        _cast_bf16_kernel,
        grid=(k // rows_per_step,),
        in_specs=[pl.BlockSpec((None, rows_per_step, n), lambda i: (layer, i, 0))],
        out_specs=pl.BlockSpec((rows_per_step, n), lambda i: (i, 0)),
        out_shape=jax.ShapeDtypeStruct((k, n), BF16),
        compiler_params=_cparams(1),
        name="cast_bf16",
    )(w)


def _mixer_out_kernel(a_ref, w_ref, h_ref, g_ref, ng_ref, sh_ref, sc_ref, *rest, route):
    if route:
        r_ref, o_ref, u_ref, route_ref = rest
    else:
        o_ref, u_ref = rest
    h_new = h_ref[...] + g_ref[...] * _dot(a_ref[...], w_ref[...])
    o_ref[...] = h_new
    u = _rms_norm(h_new, ng_ref[...]) * (1.0 + sc_ref[...]) + sh_ref[...]
    u_ref[...] = u.astype(u_ref.dtype)
    if route:
        route_ref[...] = _top2_route(u, r_ref[...])


def _mixer_out(a, w, slot, h, mods, rows, layer, norm_ffn, router=None, moe_layer=0, name="mixer_out"):
    m, k = a.shape
    d = w.shape[-1]
    g = rows.groups
    w_bf16 = _cast_bf16(w, slot)
    mod = lambda which: pl.BlockSpec((None, 1, d), lambda i: ((layer * 6 + which) * g + rows.group(i), 0, 0))
    tile = pl.BlockSpec((ROW_TILE, d), lambda i: (i, 0))
    in_specs = [pl.BlockSpec((ROW_TILE, k), lambda i: (i, 0)),
                pl.BlockSpec((k, d), lambda i: (0, 0)),
                tile, mod(2),
                pl.BlockSpec((None, 1, d), lambda i: (layer, 0, 0)), mod(3), mod(4)]
    args = [a, w_bf16, h, mods, norm_ffn, mods, mods]
    out_specs = [tile, tile]
    if router is None:
        out_shape = [jax.ShapeDtypeStruct((m, d), F32), jax.ShapeDtypeStruct((m, d), BF16)]
    else:
        in_specs.append(pl.BlockSpec((None, d, LANES), lambda i: (moe_layer, 0, 0)))
        args.append(router)
        out_specs.append(pl.BlockSpec((ROW_TILE, LANES), lambda i: (i, 0)))
        out_shape = [jax.ShapeDtypeStruct((m, d), F32), jax.ShapeDtypeStruct((m, d), F32),
                     jax.ShapeDtypeStruct((m, LANES), F32)]
    return pl.pallas_call(
        functools.partial(_mixer_out_kernel, route=router is not None),
        grid=(m // ROW_TILE,),
        in_specs=in_specs,
        out_specs=out_specs,
        out_shape=out_shape,
        compiler_params=_cparams(1),
        name=name,
    )(*args)


def _norm_mod(h, norm_w, mods, rows, layer, which_shift):
    m, d = h.shape
    return pl.pallas_call(
        _norm_mod_kernel,
        grid=(1, m // ROW_TILE),
        in_specs=[pl.BlockSpec((ROW_TILE, d), lambda n, i: (i, 0)),
                  pl.BlockSpec((None, 1, d), lambda n, i: (layer, 0, 0)),
                  _mod_spec(rows, layer, which_shift, d, lambda n: 0),
                  _mod_spec(rows, layer, which_shift + 1, d, lambda n: 0)],
        out_specs=pl.BlockSpec((ROW_TILE, d), lambda n, i: (i, 0)),
        out_shape=jax.ShapeDtypeStruct((m, d), BF16),
        compiler_params=_cparams(2),
        name="norm_mod",
    )(h, norm_w, mods, mods)


def _final_norm(h, norm_w, rows):
    d = h.shape[1]
    first = rows.n_ctx // ROW_TILE
    n_lat = rows.batch * rows.seq
    return pl.pallas_call(
        _final_norm_kernel,
        grid=(n_lat // ROW_TILE,),
        in_specs=[pl.BlockSpec((ROW_TILE, d), lambda i: (first + i, 0)),
                  pl.BlockSpec((1, d), lambda i: (0, 0))],
        out_specs=pl.BlockSpec((ROW_TILE, d), lambda i: (i, 0)),
        out_shape=jax.ShapeDtypeStruct((n_lat, d), F32),
        compiler_params=_cparams(1),
        name="final_norm",
    )(h, norm_w.reshape(1, d))


def _rope_tables(rows):
    s = rows.seq
    pos = np.arange(s)
    row = (pos // GRID_W).astype(np.float64)
    col = (pos % GRID_W).astype(np.float64)
    sec = HEAD_DIM // 2
    inv = ROPE_THETA ** (-np.arange(0, sec, 2, dtype=np.float64) / sec)
    inv = inv.astype(np.float32).astype(np.float64)
    ang = np.concatenate([row[:, None] * inv, row[:, None] * inv, col[:, None] * inv, col[:, None] * inv], axis=1)
    ang = ang.astype(np.float32).astype(np.float64)
    cos, sin = np.cos(ang), np.sin(ang)
    first_half = (np.arange(HEAD_DIM) % sec) < (sec // 2)
    sa = np.where(first_half[None, :], -sin, 0.0)
    sb = np.where(first_half[None, :], 0.0, sin)
    ident = np.zeros((rows.n_ctx, HEAD_DIM))
    mk = lambda ctx_rows, lat: jnp.asarray(np.concatenate([ctx_rows, lat], axis=0), dtype=F32)
    return mk(ident + 1.0, cos), mk(ident, sa), mk(ident, sb)


def _attn_block(q_ref, o_ref, sink, s_keys, v_keys, bias):
    q = jnp.concatenate([q_ref[:, g * HEAD_DIM:(g + 1) * HEAD_DIM] for g in range(ATTN_GROUP)], axis=0)
    s_c = _dot_nt(q, s_keys[0])
    m = jnp.maximum(jnp.max(s_c, axis=-1, keepdims=True), sink)
    if bias is not None:
        s_b = _dot_nt(q, s_keys[1]) + jnp.concatenate([bias] * ATTN_GROUP, axis=0)
        m = jnp.maximum(m, jnp.max(s_b, axis=-1, keepdims=True))
    p_c = jnp.exp2(s_c - m)
    den = jnp.sum(p_c, axis=-1, keepdims=True) + jnp.exp2(sink - m)
    o = _dot(p_c.astype(BF16), v_keys[0])
    if bias is not None:
        p_b = jnp.exp2(s_b - m)
        den = den + jnp.sum(p_b, axis=-1, keepdims=True)
        o = o + _dot(p_b.astype(BF16), v_keys[1])
    o = o * (1.0 / den)
    for g in range(ATTN_GROUP):
        o_ref[:, g * HEAD_DIM:(g + 1) * HEAD_DIM] = o[g * ATTN_BLOCK:(g + 1) * ATTN_BLOCK].astype(o_ref.dtype)


def _attn_kernel(sink_ref, q_ref, kc_ref, vc_ref, kp_ref, ko_ref, kn_ref, vp_ref, vo_ref, vn_ref, bias_ref, o_ref, *,
                 slot, n_ctx_blocks):
    hkv = pl.program_id(1)
    step = pl.program_id(2)
    rows_q = ATTN_GROUP * ATTN_BLOCK
    grp = lax.broadcasted_iota(jnp.int32, (rows_q, 1), 0) // ATTN_BLOCK
    sink = jnp.zeros((rows_q, 1), F32)
    for g in range(ATTN_GROUP):
        sink = jnp.where(grp == g, sink_ref[slot, hkv * ATTN_GROUP + g] * LOG2_E, sink)

    @pl.when(step < n_ctx_blocks)
    def _():
        _attn_block(q_ref, o_ref, sink, (kc_ref[...],), (vc_ref[...],), None)

    @pl.when(step >= n_ctx_blocks)
    def _():
        kb = jnp.concatenate([kp_ref[...], ko_ref[...], kn_ref[...]], axis=0)
        vb = jnp.concatenate([vp_ref[...], vo_ref[...], vn_ref[...]], axis=0)
        _attn_block(q_ref, o_ref, sink, (kc_ref[...], kb), (vc_ref[...], vb), bias_ref[...])


def _band_bias():
    i = np.arange(ATTN_BLOCK)[:, None]
    j = np.arange(3 * ATTN_BLOCK)[None, :]
    window = (j >= i) & (j <= i + 2 * ATTN_BLOCK)
    first = window & (j >= ATTN_BLOCK)
    last = window & (j < 2 * ATTN_BLOCK)
    return jnp.asarray(np.where(np.stack([first, window, last]), 0.0, NEG_INF), dtype=F32)


def _attention(p, sink, slot, rows):
    r = p.shape[0]
    dq = ATTN_HEADS * HEAD_DIM
    gw = ATTN_GROUP * HEAD_DIM
    kcol = dq // HEAD_DIM
    vcol = kcol + ATTN_KV_HEADS
    L, S, B = rows.ctx_len, rows.seq, rows.batch
    nb = S // ATTN_BLOCK
    nbc = L // ATTN_BLOCK
    lat0 = rows.n_ctx // ATTN_BLOCK
    assert nb >= 2

    def q_block(b, s):
        return jnp.where(s < nbc, b * nbc + s, lat0 + b * nb + (s - nbc))

    def band(col0, shift):
        return pl.BlockSpec((ATTN_BLOCK, HEAD_DIM),
                            lambda b, h, s: (lat0 + b * nb + jnp.clip(s - nbc + shift, 0, nb - 1), col0 + h))

    return pl.pallas_call(
        functools.partial(_attn_kernel, slot=slot, n_ctx_blocks=nbc),
        grid=(B, ATTN_KV_HEADS, nbc + nb),
        in_specs=[pl.BlockSpec(memory_space=pltpu.SMEM),
                  pl.BlockSpec((ATTN_BLOCK, gw), lambda b, h, s: (q_block(b, s), h)),
                  pl.BlockSpec((L, HEAD_DIM), lambda b, h, s: (b, kcol + h)),
                  pl.BlockSpec((L, HEAD_DIM), lambda b, h, s: (b, vcol + h)),
                  band(kcol, -1), band(kcol, 0), band(kcol, 1), band(vcol, -1), band(vcol, 0), band(vcol, 1),
                  pl.BlockSpec((None, ATTN_BLOCK, 3 * ATTN_BLOCK),
                               lambda b, h, s: (jnp.where(s <= nbc, 0, jnp.where(s == nbc + nb - 1, 2, 1)), 0, 0))],
        out_specs=pl.BlockSpec((ATTN_BLOCK, gw), lambda b, h, s: (q_block(b, s), h)),
        out_shape=jax.ShapeDtypeStruct((r, dq), BF16),
        compiler_params=_cparams(3),
        name="attention",
    )(sink, p, p, p, p, p, p, p, p, p, _band_bias())


def _dft_cs(n, scale):
    k = np.arange(n)
    ang = 2.0 * np.pi * ((k[:, None] * k[None, :]) % n) / n
    return np.cos(ang) * scale, np.sin(ang) * scale


def _chan_dft_kernel(u_ref, m_ref, o_ref, *, gd):
    for g in range(u_ref.shape[1] // gd):
        v = _dot(u_ref[:, g * gd:(g + 1) * gd], m_ref[...])
        o_ref[0, :, g * gd:(g + 1) * gd] = v[:, :gd].astype(o_ref.dtype)
        o_ref[1, :, g * gd:(g + 1) * gd] = v[:, gd:].astype(o_ref.dtype)


def _chan_dft(u, row0, n_rows, mat):
    d = u.shape[1]
    gd = d // FNET_GROUPS
    t0 = row0 // ROW_TILE
    return pl.pallas_call(
        functools.partial(_chan_dft_kernel, gd=gd),
        grid=(n_rows // ROW_TILE,),
        in_specs=[pl.BlockSpec((ROW_TILE, d), lambda i: (t0 + i, 0)),
                  pl.BlockSpec((gd, 2 * gd), lambda i: (0, 0))],
        out_specs=pl.BlockSpec((2, ROW_TILE, d), lambda i: (0, i, 0)),
        out_shape=jax.ShapeDtypeStruct((2, n_rows, d), BF16),
        compiler_params=_cparams(1),
        name="fnet_chan_dft",
    )(u, mat)


def _seq_dft_a_kernel(x_ref, chan_ref, m_ref, tc_ref, ts_ref, o_ref, *, gd):
    n1 = x_ref.shape[0]
    parts = [_dot(x_ref[:, g * gd:(g + 1) * gd], chan_ref[...]) for g in range(x_ref.shape[1] // gd)]
    vr = jnp.concatenate([p[:, :gd] for p in parts], axis=1).astype(BF16)
    vi = jnp.concatenate([p[:, gd:] for p in parts], axis=1).astype(BF16)
    z = _dot(m_ref[...], jnp.concatenate([vr, vi], axis=0))
    zr, zi = z[:n1], z[n1:]
    tc, ts = tc_ref[...], ts_ref[...]
    for c in range(z.shape[1] // LANES):
        sl = slice(c * LANES, (c + 1) * LANES)
        o_ref[0, :, sl] = (zr[:, sl] * tc + zi[:, sl] * ts).astype(o_ref.dtype)
        o_ref[1, :, sl] = (zi[:, sl] * tc - zr[:, sl] * ts).astype(o_ref.dtype)


def _seq_dft_c_kernel(x_ref, m_ref, o_ref):
    x = jnp.concatenate([x_ref[0], x_ref[1]], axis=0)
    o_ref[...] = _dot(m_ref[...], x).astype(o_ref.dtype)


def _fourier_tokens(u, rows):
    d = u.shape[1]
    gd = d // FNET_GROUPS
    B, S, L = rows.batch, rows.seq, rows.ctx_len
    n2 = GRID_W
    n1 = S // n2
    cc, sc = _dft_cs(gd, gd ** -0.5)
    chan = jnp.asarray(np.concatenate([cc, -sc], axis=1), dtype=BF16)

    x_lat = u[rows.n_ctx:].reshape(B, n1, n2 * d)
    c1, s1 = _dft_cs(n1, n1 ** -0.5)
    m1 = jnp.asarray(np.block([[c1, s1], [-s1, c1]]), dtype=BF16)
    ang = 2.0 * np.pi * (np.arange(n2)[:, None] * np.arange(n1)[None, :]) / S
    tw_c = jnp.asarray(np.broadcast_to(np.cos(ang)[:, :, None], (n2, n1, LANES)), dtype=F32)
    tw_s = jnp.asarray(np.broadcast_to(np.sin(ang)[:, :, None], (n2, n1, LANES)), dtype=F32)
    z = pl.pallas_call(
        functools.partial(_seq_dft_a_kernel, gd=gd),
        grid=(B, n2),
        in_specs=[pl.BlockSpec((None, n1, d), lambda b, t: (b, 0, t)),
                  pl.BlockSpec((gd, 2 * gd), lambda b, t: (0, 0)),
                  pl.BlockSpec((2 * n1, 2 * n1), lambda b, t: (0, 0)),
                  pl.BlockSpec((None, n1, LANES), lambda b, t: (t, 0, 0)),
                  pl.BlockSpec((None, n1, LANES), lambda b, t: (t, 0, 0))],
        out_specs=pl.BlockSpec((2, None, None, n1, d), lambda b, t: (0, b, t, 0, 0)),
        out_shape=jax.ShapeDtypeStruct((2, B, n2, n1, d), BF16),
        compiler_params=_cparams(2),
        name="fnet_seq_dft_a",
    )(x_lat, chan, m1, tw_c, tw_s)
    z = z.reshape(2, B, n2, n1 * d)
    c2, s2 = _dft_cs(n2, n2 ** -0.5)
    m2 = jnp.asarray(np.concatenate([c2, s2], axis=1), dtype=BF16)
    tc = 8192
    assert (n1 * d) % tc == 0
    y_lat = pl.pallas_call(
        _seq_dft_c_kernel,
        grid=(B, n1 * d // tc),
        in_specs=[pl.BlockSpec((2, None, n2, tc), lambda b, j: (0, b, 0, j)),
                  pl.BlockSpec((n2, 2 * n2), lambda b, j: (0, 0))],
        out_specs=pl.BlockSpec((None, n2, tc), lambda b, j: (b, 0, j)),
        out_shape=jax.ShapeDtypeStruct((B, n2, n1 * d), BF16),
        compiler_params=_cparams(2),
        name="fnet_seq_dft_c",
    )(z, m2)

    vc = _chan_dft(u, 0, B * L, chan).reshape(2, B, L, d)
    cl, sl = _dft_cs(L, L ** -0.5)
    ml = jnp.asarray(np.concatenate([cl, sl], axis=1), dtype=BF16)
    y_ctx = pl.pallas_call(
        _seq_dft_c_kernel,
        grid=(B, 1),
        in_specs=[pl.BlockSpec((2, None, L, d), lambda b, j: (0, b, 0, 0)),
                  pl.BlockSpec((L, 2 * L), lambda b, j: (0, 0))],
        out_specs=pl.BlockSpec((None, L, d), lambda b, j: (b, 0, 0)),
        out_shape=jax.ShapeDtypeStruct((B, L, d), BF16),
        compiler_params=_cparams(2),
        name="fnet_ctx_dft",
    )(vc, ml)
    return jnp.concatenate([y_ctx.reshape(B * L, d), y_lat.reshape(B * S, d)], axis=0)


def _hgrn_scan_kernel(q_ref, f_ref, v_ref, lb_ref, tri_ref, o_ref, st_ref, *, reverse):
    @pl.when(pl.program_id(1) == 0)
    def _():
        st_ref[...] = jnp.zeros(st_ref.shape, st_ref.dtype)

    c = HGRN_CHUNK
    n_sub = q_ref.shape[0] // c
    lb = lb_ref[...]
    tri = tri_ref[...]
    ti = lax.broadcasted_iota(jnp.int32, (c, c), 0)
    si = lax.broadcasted_iota(jnp.int32, (c, c), 1)
    keep = (si >= ti) if reverse else (si <= ti)
    order = range(n_sub - 1, -1, -1) if reverse else range(n_sub)
    for sub in order:
        rs = slice(sub * c, (sub + 1) * c)
        q = q_ref[rs, :].astype(F32)
        f = lb + (1.0 - lb) * _sigmoid(f_ref[rs, :].astype(F32))
        k = 1.0 - f
        bsum = jnp.dot(tri, jnp.log(f), preferred_element_type=F32, precision=lax.Precision.HIGHEST)
        b_end = bsum[0:1, :] if reverse else bsum[c - 1:c, :]
        decay = jnp.exp(b_end)
        q_in = (q * jnp.exp(bsum)).astype(BF16)
        k_inf = k * jnp.exp(-bsum)
        k_in = k_inf.astype(BF16)
        k_out = (k_inf * decay).astype(BF16)
        v = v_ref[rs, :]
        vt = v.astype(F32)
        for h in range(HGRN_HEADS):
            ks = slice(h * HGRN_DK, (h + 1) * HGRN_DK)
            vs = slice(h * HGRN_DV, (h + 1) * HGRN_DV)
            a = jnp.where(keep, _dot_nt(q_in[:, ks], k_in[:, ks]), 0.0)
            st = st_ref[h]
            o = _dot(a.astype(BF16), v[:, vs]) + _dot_nt(q_in[:, ks], st.astype(BF16))
            o_ref[rs, vs] = o
            st_ref[h] = st * decay[:, ks] + _dot(vt[:, vs].T.astype(BF16), k_out[:, ks])


def _hgrn_scan(pm, lb, rows, reverse):
    r = pm.shape[0]
    hk = HGRN_HEADS * HGRN_DK
    B, S, L = rows.batch, rows.seq, rows.ctx_len
    cs, ls = L // SCAN_ROWS, S // SCAN_ROWS
    direction = 1 if reverse else 0

    def row_block(b, s):
        if reverse:
            ctx = b * cs + (cs - 1 - s)
            lat = B * cs + b * ls + (ls - 1 - (s - cs))
        else:
            ctx = b * cs + s
            lat = B * cs + b * ls + (s - cs)
        return jnp.where(s < cs, ctx, lat)

    c = HGRN_CHUNK
    tri_np = np.triu(np.ones((c, c))) if reverse else np.tril(np.ones((c, c)))
    tri = jnp.asarray(tri_np, dtype=F32)
    col = lambda j: pl.BlockSpec((SCAN_ROWS, hk), lambda b, s: (row_block(b, s), j))
    return pl.pallas_call(
        functools.partial(_hgrn_scan_kernel, reverse=reverse),
        grid=(B, cs + ls),
        in_specs=[col(0), col(1 + direction), col(3),
                  pl.BlockSpec((None, 1, hk), lambda b, s: (direction, 0, 0)),
                  pl.BlockSpec((c, c), lambda b, s: (0, 0))],
        out_specs=pl.BlockSpec((SCAN_ROWS, HGRN_HEADS * HGRN_DV), lambda b, s: (row_block(b, s), 0)),
        out_shape=jax.ShapeDtypeStruct((r, HGRN_HEADS * HGRN_DV), F32),
        scratch_shapes=[pltpu.VMEM((HGRN_HEADS, HGRN_DV, HGRN_DK), F32)],
        compiler_params=_cparams(2),
        name="hgrn_scan_bwd" if reverse else "hgrn_scan_fwd",
    )(pm, pm, pm, lb, tri)


def _hgrn_readout_kernel(of_ref, ob_ref, g_ref, ng_ref, o_ref):
    for h in range(HGRN_HEADS):
        vs = slice(h * HGRN_DV, (h + 1) * HGRN_DV)
        o = of_ref[:, vs] + ob_ref[:, vs]
        o = o * lax.rsqrt(jnp.mean(o * o, axis=-1, keepdims=True) + RMS_EPS) * ng_ref[:, vs]
        o_ref[:, vs] = (o * g_ref[:, vs].astype(F32)).astype(o_ref.dtype)


def _hgrn_readout(o_f, o_b, pm, norm_g, slot):
    r, d = o_f.shape
    gcol = pm.shape[1] // d - 1
    return pl.pallas_call(
        _hgrn_readout_kernel,
        grid=(r // ROW_TILE,),
        in_specs=[pl.BlockSpec((ROW_TILE, d), lambda i: (i, 0)),
                  pl.BlockSpec((ROW_TILE, d), lambda i: (i, 0)),
                  pl.BlockSpec((ROW_TILE, d), lambda i: (i, gcol)),
                  pl.BlockSpec((None, 1, d), lambda i: (slot, 0, 0))],
        out_specs=pl.BlockSpec((ROW_TILE, d), lambda i: (i, 0)),
        out_shape=jax.ShapeDtypeStruct((r, d), BF16),
        compiler_params=_cparams(1),
        name="hgrn_readout",
    )(o_f, o_b, pm, norm_g)


def _moe_plan(route, n_rows, n_tiles):
    e_flat = jnp.concatenate([route[:, 0], route[:, 1]]).astype(jnp.int32)
    t_flat = jnp.tile(jnp.arange(n_rows, dtype=jnp.int32), 2)
    onehot = (e_flat[:, None] == jnp.arange(N_EXPERTS, dtype=jnp.int32)[None, :]).astype(jnp.int32)
    csum = jnp.cumsum(onehot, axis=0)
    counts = csum[-1]
    rank = jnp.sum(csum * onehot, axis=1) - 1
    padded = ((counts + MOE_ROW_TILE - 1) // MOE_ROW_TILE) * MOE_ROW_TILE
    ends = jnp.cumsum(padded)
    starts = ends - padded
    dest = jnp.sum(starts[None, :] * onehot, axis=1) + rank
    p_rows = n_tiles * MOE_ROW_TILE
    row_token = jnp.zeros((p_rows,), jnp.int32).at[dest].set(t_flat)
    n_used = (ends[-1] // MOE_ROW_TILE).astype(jnp.int32)
    tile = jnp.arange(n_tiles, dtype=jnp.int32)
    te = jnp.sum((ends[None, :] <= (tile * MOE_ROW_TILE)[:, None]).astype(jnp.int32), axis=1)
    te = jnp.minimum(te, N_EXPERTS - 1)
    te_last = jnp.sum(jnp.where(tile == n_used - 1, te, 0))
    te = jnp.where(tile < n_used, te, te_last).astype(jnp.int32)
    return row_token, dest, te, n_used.reshape(1)


def _moe_gather_kernel(tok_ref, nu_ref, src_ref, o_ref, buf_ref, sem):
    i = pl.program_id(0)
    n_used = nu_ref[0]

    def row_copy(slot, r, t):
        return pltpu.make_async_copy(src_ref.at[pl.ds(t, 1), :], buf_ref.at[slot, pl.ds(r, 1), :], sem.at[slot])

    def issue_tile(tile):
        slot = tile % 2
        base = tile * MOE_ROW_TILE

        def issue(r, carry):
            row_copy(slot, r, tok_ref[base + r]).start()
            return carry

        lax.fori_loop(0, MOE_ROW_TILE, issue, 0, unroll=8)

    @pl.when(i == 0)
    def _():
        issue_tile(i)

    @pl.when(i + 1 < n_used)
    def _():
        issue_tile(i + 1)

    @pl.when(i < n_used)
    def _():
        slot = i % 2

        def drain(r, carry):
            row_copy(slot, r, 0).wait()
            return carry

        lax.fori_loop(0, MOE_ROW_TILE, drain, 0, unroll=8)
        o_ref[...] = buf_ref[slot].astype(o_ref.dtype)

    @pl.when(i >= n_used)
    def _():
        o_ref[...] = jnp.zeros(o_ref.shape, o_ref.dtype)


def _moe_gather(row_token, n_used, u, n_tiles):
    d = u.shape[1]
    return pl.pallas_call(
        _moe_gather_kernel,
        grid_spec=pltpu.PrefetchScalarGridSpec(
            num_scalar_prefetch=2,
            grid=(n_tiles,),
            in_specs=[pl.BlockSpec(memory_space=pl.ANY)],
            out_specs=pl.BlockSpec((MOE_ROW_TILE, d), lambda i, tok, nu: (i, 0)),
            scratch_shapes=[pltpu.VMEM((2, MOE_ROW_TILE, d), F32), pltpu.SemaphoreType.DMA((2,))]),
        out_shape=jax.ShapeDtypeStruct((n_tiles * MOE_ROW_TILE, d), BF16),
        compiler_params=_cparams(1),
        name="moe_gather",
    )(row_token, n_used, u)


def _expert_changed(te_ref):
    i = pl.program_id(1)
    return (i == 0) | (te_ref[i] != te_ref[jnp.maximum(i - 1, 0)])


def _gmm_swiglu_kernel(te_ref, nu_ref, x_ref, wa_ref, wb_ref, w2_ref, o_ref, w2bf_ref, wbfa_ref, wbfb_ref):
    i = pl.program_id(1)

    @pl.when(_expert_changed(te_ref))
    def _():
        wbfa_ref[...] = wa_ref[...].astype(BF16)
        wbfb_ref[...] = wb_ref[...].astype(BF16)

    @pl.when(i < nu_ref[0])
    def _():
        x = x_ref[...]
        o_ref[...] = (_silu(_dot(x, wbfa_ref[...])) * _dot(x, wbfb_ref[...])).astype(o_ref.dtype)

    @pl.when(i >= nu_ref[0])
    def _():
        o_ref[...] = jnp.zeros(o_ref.shape, o_ref.dtype)

    w2bf_ref[...] = w2_ref[...].astype(BF16)


def _gmm_down_kernel(te_ref, nu_ref, x_ref, w_ref, o_ref):
    i = pl.program_id(1)

    @pl.when(i < nu_ref[0])
    def _():
        o_ref[...] = _dot(x_ref[...], w_ref[...])

    @pl.when(i >= nu_ref[0])
    def _():
        o_ref[...] = jnp.zeros(o_ref.shape, o_ref.dtype)


def _moe_experts(xg, te, n_used, w13, w2, layer, n_tiles):
    p_rows, d = xg.shape
    n_layers, n_exp, f, d_out = w2.shape
    tn = COL_TILE
    nf = f // tn
    used = lambda i, nu: jnp.minimum(i, nu[0] - 1)
    ch = _side_cast_rows(n_exp * f, nf * n_tiles)
    n_chunks = n_exp * f // ch
    chunk = lambda n, i: jnp.minimum(n * n_tiles + i, n_chunks - 1)
    hidden, w2_bf16 = pl.pallas_call(
        _gmm_swiglu_kernel,
        grid_spec=pltpu.PrefetchScalarGridSpec(
            num_scalar_prefetch=2,
            grid=(nf, n_tiles),
            in_specs=[pl.BlockSpec((MOE_ROW_TILE, d), lambda n, i, te, nu: (used(i, nu), 0)),
                      pl.BlockSpec((None, None, d, tn), lambda n, i, te, nu: (layer, te[i], 0, n)),
                      pl.BlockSpec((None, None, d, tn), lambda n, i, te, nu: (layer, te[i], 0, n + nf)),
                      pl.BlockSpec((None, ch, d_out), lambda n, i, te, nu: (layer, chunk(n, i), 0))],
            out_specs=[pl.BlockSpec((MOE_ROW_TILE, tn), lambda n, i, te, nu: (i, n)),
                       pl.BlockSpec((ch, d_out), lambda n, i, te, nu: (chunk(n, i), 0))],
            scratch_shapes=[pltpu.VMEM((d, tn), BF16), pltpu.VMEM((d, tn), BF16)]),
        out_shape=[jax.ShapeDtypeStruct((p_rows, f), BF16), jax.ShapeDtypeStruct((n_exp * f, d_out), BF16)],
        compiler_params=_cparams(2),
        name="moe_w13",
    )(te, n_used, xg, w13, w13, w2.reshape(n_layers, n_exp * f, d_out))
    tn2 = MOE_W2_COL_TILE
    return pl.pallas_call(
        _gmm_down_kernel,
        grid_spec=pltpu.PrefetchScalarGridSpec(
            num_scalar_prefetch=2,
            grid=(d_out // tn2, n_tiles),
            in_specs=[pl.BlockSpec((MOE_ROW_TILE, f), lambda n, i, te, nu: (used(i, nu), 0)),
                      pl.BlockSpec((None, f, tn2), lambda n, i, te, nu: (te[i], 0, n))],
            out_specs=pl.BlockSpec((MOE_ROW_TILE, tn2), lambda n, i, te, nu: (i, n)),
            scratch_shapes=[]),
        out_shape=jax.ShapeDtypeStruct((p_rows, d_out), F32),
        compiler_params=_cparams(2),
        name="moe_w2",
    )(te, n_used, hidden, w2_bf16.reshape(n_exp, f, d_out))


def _moe_combine_kernel(p1_ref, p2_ref, y_ref, h_ref, g_ref, route_ref, *rest, mode):
    if mode == "final":
        ng_ref, o_ref, buf_ref, sem = rest
    else:
        ng_ref, sh_ref, sc_ref, o_ref, u_ref, buf_ref, sem = rest
    i = pl.program_id(0)

    def row_copy(slot, k, r, p):
        return pltpu.make_async_copy(y_ref.at[pl.ds(p, 1), :], buf_ref.at[slot, k, pl.ds(r, 1), :], sem.at[slot])

    def issue_tile(tile):
        slot = tile % 2
        base = tile * COMBINE_ROW_TILE

        def issue(r, carry):
            row_copy(slot, 0, r, p1_ref[base + r]).start()
            row_copy(slot, 1, r, p2_ref[base + r]).start()
            return carry

        lax.fori_loop(0, COMBINE_ROW_TILE, issue, 0, unroll=8)

    @pl.when(i == 0)
    def _():
        issue_tile(i)

    @pl.when(i + 1 < pl.num_programs(0))
    def _():
        issue_tile(i + 1)

    slot = i % 2

    def drain(r, carry):
        row_copy(slot, 0, r, 0).wait()
        row_copy(slot, 1, r, 0).wait()
        return carry

    lax.fori_loop(0, COMBINE_ROW_TILE, drain, 0, unroll=8)
    w1 = route_ref[:, 2:3]
    w2 = route_ref[:, 3:4]
    h_new = h_ref[...] + g_ref[...] * (w1 * buf_ref[slot, 0] + w2 * buf_ref[slot, 1])
    if mode == "final":
        o_ref[...] = _rms_norm(h_new, ng_ref[...])
    else:
        o_ref[...] = h_new
        u_ref[...] = (_rms_norm(h_new, ng_ref[...]) * (1.0 + sc_ref[...]) + sh_ref[...]).astype(u_ref.dtype)


def _moe_combine(pos1, pos2, y, h, mods, route, rows, mod_layer, row0, post):
    d = h.shape[1]
    r = h.shape[0] - row0
    g = rows.groups
    tr = COMBINE_ROW_TILE
    t0 = row0 // tr
    mod = lambda layer, which: pl.BlockSpec(
        (None, 1, d), lambda i, p1, p2: ((layer * 6 + which) * g + rows.group(t0 + i, tr), 0, 0))
    tile = pl.BlockSpec((tr, d), lambda i, p1, p2: (i, 0))
    in_specs = [pl.BlockSpec(memory_space=pl.ANY),
                pl.BlockSpec((tr, d), lambda i, p1, p2: (t0 + i, 0)),
                mod(mod_layer, 5),
                pl.BlockSpec((tr, LANES), lambda i, p1, p2: (t0 + i, 0))]
    if post[0] == "final":
        in_specs.append(pl.BlockSpec((1, d), lambda i, p1, p2: (0, 0)))
        extra = (post[1],)
        out_specs, out_shape = tile, jax.ShapeDtypeStruct((r, d), F32)
    else:
        _, norm_w, nxt = post
        in_specs += [pl.BlockSpec((None, 1, d), lambda i, p1, p2: (nxt, 0, 0)), mod(nxt, 0), mod(nxt, 1)]
        extra = (norm_w, mods, mods)
        out_specs = [tile, tile]
        out_shape = [jax.ShapeDtypeStruct((r, d), F32), jax.ShapeDtypeStruct((r, d), BF16)]
    return pl.pallas_call(
        functools.partial(_moe_combine_kernel, mode=post[0]),
        grid_spec=pltpu.PrefetchScalarGridSpec(
            num_scalar_prefetch=2,
            grid=(r // tr,),
            in_specs=in_specs,
            out_specs=out_specs,
            scratch_shapes=[pltpu.VMEM((2, 2, tr, d), F32), pltpu.SemaphoreType.DMA((2,))]),
        out_shape=out_shape,
        compiler_params=_cparams(1),
        name="moe_combine",
    )(pos1, pos2, y, h, mods, route, *extra)


def _moe_ffn(h, u, route, mods, rows, layer, w13, w2, moe_layer, row0, post):
    r = h.shape[0] - row0
    n_tiles = -(-(2 * r + N_EXPERTS * (MOE_ROW_TILE - 1)) // MOE_ROW_TILE)
    row_token, dest, te, n_used = _moe_plan(route[row0:], r, n_tiles)
    xg = _moe_gather(row_token + row0, n_used, u, n_tiles)
    y = _moe_experts(xg, te, n_used, w13, w2, moe_layer, n_tiles)
    return _moe_combine(dest[:r], dest[r:], y, h, mods, route, rows, layer, row0, post)


def _hgrn_lower_bounds(lb_logits, layer):
    gamma = jax.nn.softmax(lb_logits.astype(F32), axis=0)
    lb = jnp.cumsum(gamma, axis=0) - gamma[0]
    return lb[layer]


def kernel(x, c, ctx, c_ctx, ada_w, ada_b, norm_mix, norm_ffn, norm_final, attn_wqkv, attn_wo, attn_sink, fnet_wo,
           hgrn_win, hgrn_lb, hgrn_norm, hgrn_wo, ffn_w13, ffn_w2, moe_router, moe_w13, moe_w2):
    B, S, D = x.shape
    L = ctx.shape[1]
    depth = ada_w.shape[0]
    rows = _Rows(B, S, L)
    G = rows.groups

    cond = jnp.concatenate([c_ctx[None, :], c], axis=0)
    cond = jnp.pad(_silu(cond), ((0, 16 - G), (0, 0))).astype(BF16)
    mods = _ada_mods(cond, ada_w, ada_b)
    mods = mods[:, :G, :].reshape(depth, G, 6, D).transpose(0, 2, 1, 3).reshape(depth * 6 * G, 1, D)

    h = jnp.concatenate([ctx.reshape(B * L, D), x.reshape(B * S, D)], axis=0)
    norm_mix3 = norm_mix.reshape(depth, 1, D)
    norm_ffn3 = norm_ffn.reshape(depth, 1, D)
    rope_tables = _rope_tables(rows)
    router_pad = jnp.pad(moe_router, ((0, 0), (0, 0), (0, LANES - moe_router.shape[-1])))

    u = None
    for i in range(depth):
        kind, slot = i % N_MIXERS, i // N_MIXERS
        j = i // 2
        dense = i % 2 == 0
        if u is None:
            u = _norm_mod(h, norm_mix3, mods, rows, i, 0)
        if kind == 0:
            p = _mm_qkv_rope(u, attn_wqkv, slot, rope_tables, rows)
            o, w_out, name = _attention(p, attn_sink, slot, rows), attn_wo, "attn_out"
        elif kind == 1:
            o, w_out, name = _fourier_tokens(u, rows), fnet_wo, "fnet_out"
        else:
            lb = _hgrn_lower_bounds(hgrn_lb, i).reshape(2, 1, HGRN_HEADS * HGRN_DK)
            pm = _mm_hgrn_in(u, hgrn_win, slot)
            o_f = _hgrn_scan(pm, lb, rows, reverse=False)
            o_b = _hgrn_scan(pm, lb, rows, reverse=True)
            o, w_out, name = _hgrn_readout(o_f, o_b, pm, hgrn_norm.reshape(-1, 1, D), slot), hgrn_wo, "hgrn_out"
        u = None

        if dense:
            h, v = _mixer_out(o, w_out, slot, h, mods, rows, i, norm_ffn3, name=name)
            g, w2_bf16 = _mm_swiglu(v, ffn_w13, ffn_w2, j)
            h = _mm_resid_bf16w(g, w2_bf16, h, mods, rows, i, 5, FFN_W2_COL_TILE, name="ffn_out")
            continue
        h, v, route = _mixer_out(o, w_out, slot, h, mods, rows, i, norm_ffn3, router_pad, j, name=name)
        if i == depth - 1:
            out = _moe_ffn(h, v, route, mods, rows, i, moe_w13, moe_w2, j, rows.n_ctx,
                           ("final", norm_final.reshape(1, D)))
            return out.reshape(B, S, D)
        h, u = _moe_ffn(h, v, route, mods, rows, i, moe_w13, moe_w2, j, 0, ("next", norm_mix3, i + 1))

    return _final_norm(h, norm_final, rows).reshape(B, S, D)
```

```python
import functools

import numpy as np
import jax
import jax.numpy as jnp
from jax import lax
from jax.experimental import pallas as pl
from jax.experimental.pallas import tpu as pltpu

F32 = jnp.float32
BF16 = jnp.bfloat16

N_MIXERS = 3
RMS_EPS = 1e-6
NEG_INF = -1e30
LOG2_E = 1.4426950408889634
GRID_W = 64
ATTN_HEADS = 16
ATTN_KV_HEADS = 4
ATTN_GROUP = ATTN_HEADS // ATTN_KV_HEADS
HEAD_DIM = 128
ATTN_BLOCK = 128
ROPE_THETA = 10000.0
FNET_GROUPS = 8
HGRN_HEADS = 16
HGRN_DK = 128
HGRN_DV = 128
HGRN_CHUNK = 64
N_EXPERTS = 8

LANES = 128
ROW_TILE = 512
COL_TILE = 1024
SWIGLU_COL_TILE = 512
FFN_W2_COL_TILE = 1024
MOE_W2_COL_TILE = 1024
MOE_ROW_TILE = 512
COMBINE_ROW_TILE = 256
SCAN_ROWS = 256
DFT_POS_BLOCK = 16
VMEM_LIMIT_BYTES = 56 * 1024 * 1024


def _cparams(n_axes):
    return pltpu.CompilerParams(dimension_semantics=("arbitrary",) * n_axes,
                                vmem_limit_bytes=VMEM_LIMIT_BYTES)


def _dot(a, b):
    return jnp.dot(a, b, preferred_element_type=F32)


def _dot_nt(a, b):
    return lax.dot_general(a, b, (((1,), (1,)), ((), ())), preferred_element_type=F32)


def _sigmoid(x):
    return 1.0 / (1.0 + jnp.exp(-x))


def _silu(x):
    return x * _sigmoid(x)


def _pack_bf16_pairs(x):
    n = x.shape[1] // 2
    lo = lax.bitcast_convert_type(x[:, :n].astype(BF16).astype(F32), jnp.uint32)
    hi = lax.bitcast_convert_type(x[:, n:].astype(BF16).astype(F32), jnp.uint32)
    return (hi & jnp.uint32(0xFFFF0000)) | (lo >> 16)


def _unpack_bf16_pairs(p):
    lo = lax.bitcast_convert_type(p << 16, F32)
    hi = lax.bitcast_convert_type(p & jnp.uint32(0xFFFF0000), F32)
    return jnp.concatenate([lo, hi], axis=1)


def _ada_kernel(a_ref, w_ref, b_ref, o_ref):
    o_ref[...] = _dot(a_ref[...], w_ref[...].astype(BF16)) + b_ref[...]


def _ada_mods(cond_rows, ada_w, ada_b):
    depth, d, n6 = ada_w.shape
    rows = cond_rows.shape[0]
    tn = 1024
    return pl.pallas_call(
        _ada_kernel,
        grid=(depth, n6 // tn),
        in_specs=[pl.BlockSpec((rows, d), lambda l, n: (0, 0)),
                  pl.BlockSpec((None, d, tn), lambda l, n: (l, 0, n)),
                  pl.BlockSpec((None, 1, tn), lambda l, n: (l, 0, n))],
        out_specs=pl.BlockSpec((None, rows, tn), lambda l, n: (l, 0, n)),
        out_shape=jax.ShapeDtypeStruct((depth, rows, n6), F32),
        compiler_params=_cparams(2),
        name="ada_mods",
    )(cond_rows, ada_w, ada_b.reshape(depth, 1, n6))


def _rms_norm(h, gain):
    return h * lax.rsqrt(jnp.mean(h * h, axis=-1, keepdims=True) + RMS_EPS) * gain


def _norm_mod_kernel(h_ref, g_ref, sh_ref, sc_ref, o_ref):
    o_ref[...] = (_rms_norm(h_ref[...], g_ref[...]) * (1.0 + sc_ref[...]) + sh_ref[...]).astype(o_ref.dtype)


def _top2_route(u, router):
    logits = _dot(u.astype(BF16), router)
    lane = lax.broadcasted_iota(jnp.int32, logits.shape, 1)
    l1 = jnp.where(lane < N_EXPERTS, logits, NEG_INF)
    m1 = jnp.max(l1, axis=-1, keepdims=True)
    i1 = jnp.min(jnp.where(l1 == m1, lane, LANES), axis=-1, keepdims=True)
    l2 = jnp.where(lane == i1, NEG_INF, l1)
    m2 = jnp.max(l2, axis=-1, keepdims=True)
    i2 = jnp.min(jnp.where(l2 == m2, lane, LANES), axis=-1, keepdims=True)
    e2 = jnp.exp(m2 - m1)
    w1 = 1.0 / (1.0 + e2)
    w2 = e2 * w1
    return jnp.where(lane == 0, i1.astype(F32),
                     jnp.where(lane == 1, i2.astype(F32), jnp.where(lane == 2, w1, jnp.where(lane == 3, w2, 0.0))))


def _final_norm_kernel(h_ref, g_ref, o_ref):
    o_ref[...] = _rms_norm(h_ref[...], g_ref[...])


def _cast_weight(w_ref, wbf_ref):
    @pl.when(pl.program_id(1) == 0)
    def _():
        wbf_ref[...] = w_ref[...].astype(BF16)


def _mm_rope_kernel(a_ref, w_ref, cos_ref, sa_ref, sb_ref, o_ref, wbf_ref, *, n_col_tiles, n_q_heads, n_rope_heads,
                    q_scale):
    _cast_weight(w_ref, wbf_ref)
    acc = _dot(a_ref[...], wbf_ref[...])
    heads_per_tile = acc.shape[1] // HEAD_DIM
    for tile in range(n_col_tiles):
        @pl.when(pl.program_id(0) == tile)
        def _(tile=tile):
            for c in range(heads_per_tile):
                head = tile * heads_per_tile + c
                t = acc[:, c * HEAD_DIM:(c + 1) * HEAD_DIM]
                if head < n_rope_heads:
                    t = t * cos_ref[...] + pltpu.roll(t, HEAD_DIM - 32, 1) * sa_ref[...] + pltpu.roll(t, 32, 1) * sb_ref[...]
                if head < n_q_heads:
                    t = t * q_scale
                o_ref[:, c * HEAD_DIM:(c + 1) * HEAD_DIM] = t.astype(o_ref.dtype)


def _mm_swiglu_kernel(a_ref, wa_ref, wb_ref, w2_ref, o_ref, w2bf_ref, wbfa_ref, wbfb_ref):
    _cast_weight(wa_ref, wbfa_ref)
    _cast_weight(wb_ref, wbfb_ref)
    a = a_ref[...]
    ga = _dot(a, wbfa_ref[...])
    gb = _dot(a, wbfb_ref[...])
    o_ref[...] = (_silu(ga) * gb).astype(o_ref.dtype)
    w2bf_ref[...] = w2_ref[...].astype(BF16)


def _mm_hgrn_in_kernel(a_ref, w_ref, o_ref, wbf_ref, *, tiles_per_segment):
    _cast_weight(w_ref, wbf_ref)
    seg = pl.program_id(0) // tiles_per_segment
    acc = _dot(a_ref[...], wbf_ref[...])

    @pl.when(seg == 0)
    def _():
        o_ref[...] = _silu(acc).astype(o_ref.dtype)

    @pl.when(seg == 4)
    def _():
        o_ref[...] = _sigmoid(acc).astype(o_ref.dtype)

    @pl.when((seg != 0) & (seg != 4))
    def _():
        o_ref[...] = acc.astype(o_ref.dtype)


def _mm_resid_bf16w_kernel(a_ref, w_ref, h_ref, g_ref, o_ref):
    o_ref[...] = h_ref[...] + g_ref[...] * _dot(a_ref[...], w_ref[...])


def _side_cast_rows(total_rows, n_steps):
    for rows in range(16, total_rows + 1, 16):
        if total_rows % rows == 0 and total_rows // rows <= n_steps:
            return rows
    raise ValueError((total_rows, n_steps))


class _Rows:
    def __init__(self, batch, seq, ctx_len):
        self.batch, self.seq, self.ctx_len = batch, seq, ctx_len
        self.n_ctx = batch * ctx_len
        self.n_rows = self.n_ctx + batch * seq
        self.groups = 1 + batch
        assert self.n_ctx % ROW_TILE == 0 and seq % ROW_TILE == 0, (batch, seq, ctx_len)
        assert ctx_len % SCAN_ROWS == 0 and seq % SCAN_ROWS == 0 and ctx_len % ATTN_BLOCK == 0

    def group(self, tile, tile_rows=ROW_TILE):
        ctx_tiles = self.n_ctx // tile_rows
        per_batch = self.seq // tile_rows
        return jnp.where(tile < ctx_tiles, 0, 1 + (tile - ctx_tiles) // per_batch)


def _mod_spec(rows, layer, which, width, col_of, tile_rows=ROW_TILE, tile0=0):
    g = rows.groups
    return pl.BlockSpec((None, 1, width),
                        lambda n, i: ((layer * 6 + which) * g + rows.group(tile0 + i, tile_rows), 0, col_of(n)))


def _mm_qkv_rope(a, w, layer, tables, rows):
    m, k = a.shape
    n_total = w.shape[-1]
    tn = COL_TILE
    ctx_tiles = rows.n_ctx // ROW_TILE
    seq_tiles = rows.seq // ROW_TILE
    tab = lambda n, i: (jnp.where(i < ctx_tiles, i, ctx_tiles + (i - ctx_tiles) % seq_tiles), 0)
    kern = functools.partial(_mm_rope_kernel, n_col_tiles=n_total // tn, n_q_heads=ATTN_HEADS,
                             n_rope_heads=ATTN_HEADS + ATTN_KV_HEADS, q_scale=LOG2_E * HEAD_DIM ** -0.5)
    return pl.pallas_call(
        kern,
        grid=(n_total // tn, m // ROW_TILE),
        in_specs=[pl.BlockSpec((ROW_TILE, k), lambda n, i: (i, 0)),
                  pl.BlockSpec((None, k, tn), lambda n, i: (layer, 0, n)),
                  pl.BlockSpec((ROW_TILE, HEAD_DIM), tab),
                  pl.BlockSpec((ROW_TILE, HEAD_DIM), tab),
                  pl.BlockSpec((ROW_TILE, HEAD_DIM), tab)],
        out_specs=pl.BlockSpec((ROW_TILE, tn), lambda n, i: (i, n)),
        out_shape=jax.ShapeDtypeStruct((m, n_total), BF16),
        scratch_shapes=[pltpu.VMEM((k, tn), BF16)],
        compiler_params=_cparams(2),
        name="mm_qkv_rope",
    )(a, w, *tables)


def _mm_swiglu(a, w13, w2, layer):
    m, k = a.shape
    f = w13.shape[-1] // 2
    d_out = w2.shape[-1]
    tn = SWIGLU_COL_TILE
    nf = f // tn
    n_row_tiles = m // ROW_TILE
    ch = _side_cast_rows(f, nf * n_row_tiles)
    n_chunks = f // ch
    chunk = lambda n, i: jnp.minimum(n * n_row_tiles + i, n_chunks - 1)
    return pl.pallas_call(
        _mm_swiglu_kernel,
        grid=(nf, n_row_tiles),
        in_specs=[pl.BlockSpec((ROW_TILE, k), lambda n, i: (i, 0)),
                  pl.BlockSpec((None, k, tn), lambda n, i: (layer, 0, n)),
                  pl.BlockSpec((None, k, tn), lambda n, i: (layer, 0, n + nf)),
                  pl.BlockSpec((None, ch, d_out), lambda n, i: (layer, chunk(n, i), 0))],
        out_specs=[pl.BlockSpec((ROW_TILE, tn), lambda n, i: (i, n)),
                   pl.BlockSpec((ch, d_out), lambda n, i: (chunk(n, i), 0))],
        out_shape=[jax.ShapeDtypeStruct((m, f), BF16), jax.ShapeDtypeStruct((f, d_out), BF16)],
        scratch_shapes=[pltpu.VMEM((k, tn), BF16), pltpu.VMEM((k, tn), BF16)],
        compiler_params=_cparams(2),
        name="mm_swiglu",
    )(a, w13, w13, w2)


def _mm_hgrn_in(a, w, layer):
    m, k = a.shape
    n_total = w.shape[-1]
    tn = COL_TILE
    seg = HGRN_HEADS * HGRN_DK
    assert n_total == 5 * seg and seg % tn == 0
    return pl.pallas_call(
        functools.partial(_mm_hgrn_in_kernel, tiles_per_segment=seg // tn),
        grid=(n_total // tn, m // ROW_TILE),
        in_specs=[pl.BlockSpec((ROW_TILE, k), lambda n, i: (i, 0)),
                  pl.BlockSpec((None, k, tn), lambda n, i: (layer, 0, n))],
        out_specs=pl.BlockSpec((ROW_TILE, tn), lambda n, i: (i, n)),
        out_shape=jax.ShapeDtypeStruct((m, n_total), BF16),
        scratch_shapes=[pltpu.VMEM((k, tn), BF16)],
        compiler_params=_cparams(2),
        name="hgrn_in",
    )(a, w)


def _mm_resid_bf16w(a, w_bf16, h, mods, rows, mod_layer, which, tn, name):
    m, k = a.shape
    n_total = w_bf16.shape[-1]
    return pl.pallas_call(
        _mm_resid_bf16w_kernel,
        grid=(n_total // tn, m // ROW_TILE),
        in_specs=[pl.BlockSpec((ROW_TILE, k), lambda n, i: (i, 0)),
                  pl.BlockSpec((k, tn), lambda n, i: (0, n)),
                  pl.BlockSpec((ROW_TILE, tn), lambda n, i: (i, n)),
                  _mod_spec(rows, mod_layer, which, tn, lambda n: n)],
        out_specs=pl.BlockSpec((ROW_TILE, tn), lambda n, i: (i, n)),
        out_shape=jax.ShapeDtypeStruct((m, n_total), F32),
        compiler_params=_cparams(2),
        name=name,
    )(a, w_bf16, h, mods)


def _cast_bf16_kernel(w_ref, o_ref):
    o_ref[...] = w_ref[...].astype(BF16)


def _cast_bf16(w, layer):
    _, k, n = w.shape
    rows_per_step = 256
    return pl.pallas_call(
        _cast_bf16_kernel,
        grid=(k // rows_per_step,),
        in_specs=[pl.BlockSpec((None, rows_per_step, n), lambda i: (layer, i, 0))],
        out_specs=pl.BlockSpec((rows_per_step, n), lambda i: (i, 0)),
        out_shape=jax.ShapeDtypeStruct((k, n), BF16),
        compiler_params=_cparams(1),
        name="cast_bf16",
    )(w)


def _mixer_out_kernel(a_ref, w_ref, h_ref, g_ref, ng_ref, sh_ref, sc_ref, *rest, route):
    if route:
        r_ref, o_ref, u_ref, route_ref = rest
    else:
        o_ref, u_ref = rest
    h_new = h_ref[...] + g_ref[...] * _dot(a_ref[...], w_ref[...])
    o_ref[...] = h_new
    u = _rms_norm(h_new, ng_ref[...]) * (1.0 + sc_ref[...]) + sh_ref[...]
    if route:
        u_ref[...] = _pack_bf16_pairs(u)
        route_ref[...] = _top2_route(u, r_ref[...])
    else:
        u_ref[...] = u.astype(u_ref.dtype)


def _mixer_out(a, w, slot, h, mods, rows, layer, norm_ffn, router=None, moe_layer=0, name="mixer_out"):
    m, k = a.shape
    d = w.shape[-1]
    g = rows.groups
    w_bf16 = _cast_bf16(w, slot)
    mod = lambda which: pl.BlockSpec((None, 1, d), lambda i: ((layer * 6 + which) * g + rows.group(i), 0, 0))
    tile = pl.BlockSpec((ROW_TILE, d), lambda i: (i, 0))
    in_specs = [pl.BlockSpec((ROW_TILE, k), lambda i: (i, 0)),
                pl.BlockSpec((k, d), lambda i: (0, 0)),
                tile, mod(2),
                pl.BlockSpec((None, 1, d), lambda i: (layer, 0, 0)), mod(3), mod(4)]
    args = [a, w_bf16, h, mods, norm_ffn, mods, mods]
    if router is None:
        out_specs = [tile, tile]
        out_shape = [jax.ShapeDtypeStruct((m, d), F32), jax.ShapeDtypeStruct((m, d), BF16)]
    else:
        in_specs.append(pl.BlockSpec((None, d, LANES), lambda i: (moe_layer, 0, 0)))
        args.append(router)
        out_specs = [tile, pl.BlockSpec((ROW_TILE, d // 2), lambda i: (i, 0)),
                     pl.BlockSpec((ROW_TILE, LANES), lambda i: (i, 0))]
        out_shape = [jax.ShapeDtypeStruct((m, d), F32), jax.ShapeDtypeStruct((m, d // 2), jnp.uint32),
                     jax.ShapeDtypeStruct((m, LANES), F32)]
    return pl.pallas_call(
        functools.partial(_mixer_out_kernel, route=router is not None),
        grid=(m // ROW_TILE,),
        in_specs=in_specs,
        out_specs=out_specs,
        out_shape=out_shape,
        compiler_params=_cparams(1),
        name=name,
    )(*args)


def _norm_mod(h, norm_w, mods, rows, layer, which_shift):
    m, d = h.shape
    return pl.pallas_call(
        _norm_mod_kernel,
        grid=(1, m // ROW_TILE),
        in_specs=[pl.BlockSpec((ROW_TILE, d), lambda n, i: (i, 0)),
                  pl.BlockSpec((None, 1, d), lambda n, i: (layer, 0, 0)),
                  _mod_spec(rows, layer, which_shift, d, lambda n: 0),
                  _mod_spec(rows, layer, which_shift + 1, d, lambda n: 0)],
        out_specs=pl.BlockSpec((ROW_TILE, d), lambda n, i: (i, 0)),
        out_shape=jax.ShapeDtypeStruct((m, d), BF16),
        compiler_params=_cparams(2),
        name="norm_mod",
    )(h, norm_w, mods, mods)


def _final_norm(h, norm_w, rows):
    d = h.shape[1]
    first = rows.n_ctx // ROW_TILE
    n_lat = rows.batch * rows.seq
    return pl.pallas_call(
        _final_norm_kernel,
        grid=(n_lat // ROW_TILE,),
        in_specs=[pl.BlockSpec((ROW_TILE, d), lambda i: (first + i, 0)),
                  pl.BlockSpec((1, d), lambda i: (0, 0))],
        out_specs=pl.BlockSpec((ROW_TILE, d), lambda i: (i, 0)),
        out_shape=jax.ShapeDtypeStruct((n_lat, d), F32),
        compiler_params=_cparams(1),
        name="final_norm",
    )(h, norm_w.reshape(1, d))


def _rope_tables(rows):
    s = rows.seq
    pos = np.arange(s)
    row = (pos // GRID_W).astype(np.float64)
    col = (pos % GRID_W).astype(np.float64)
    sec = HEAD_DIM // 2
    inv = ROPE_THETA ** (-np.arange(0, sec, 2, dtype=np.float64) / sec)
    inv = inv.astype(np.float32).astype(np.float64)
    ang = np.concatenate([row[:, None] * inv, row[:, None] * inv, col[:, None] * inv, col[:, None] * inv], axis=1)
    ang = ang.astype(np.float32).astype(np.float64)
    cos, sin = np.cos(ang), np.sin(ang)
    first_half = (np.arange(HEAD_DIM) % sec) < (sec // 2)
    sa = np.where(first_half[None, :], -sin, 0.0)
    sb = np.where(first_half[None, :], 0.0, sin)
    ident = np.zeros((rows.n_ctx, HEAD_DIM))
    mk = lambda ctx_rows, lat: jnp.asarray(np.concatenate([ctx_rows, lat], axis=0), dtype=F32)
    return mk(ident + 1.0, cos), mk(ident, sa), mk(ident, sb)


def _attn_block(q_ref, o_ref, sink, s_keys, v_keys, bias):
    q = jnp.concatenate([q_ref[:, g * HEAD_DIM:(g + 1) * HEAD_DIM] for g in range(ATTN_GROUP)], axis=0)
    s_c = _dot_nt(q, s_keys[0])
    m = jnp.maximum(jnp.max(s_c, axis=-1, keepdims=True), sink)
    if bias is not None:
        s_b = _dot_nt(q, s_keys[1]) + jnp.concatenate([bias] * ATTN_GROUP, axis=0)
        m = jnp.maximum(m, jnp.max(s_b, axis=-1, keepdims=True))
    p_c = jnp.exp2(s_c - m)
    den = jnp.sum(p_c, axis=-1, keepdims=True) + jnp.exp2(sink - m)
    o = _dot(p_c.astype(BF16), v_keys[0])
    if bias is not None:
        p_b = jnp.exp2(s_b - m)
        den = den + jnp.sum(p_b, axis=-1, keepdims=True)
        o = o + _dot(p_b.astype(BF16), v_keys[1])
    o = o * (1.0 / den)
    for g in range(ATTN_GROUP):
        o_ref[:, g * HEAD_DIM:(g + 1) * HEAD_DIM] = o[g * ATTN_BLOCK:(g + 1) * ATTN_BLOCK].astype(o_ref.dtype)


def _attn_kernel(sink_ref, q_ref, kc_ref, vc_ref, kp_ref, ko_ref, kn_ref, vp_ref, vo_ref, vn_ref, bias_ref, o_ref, *,
                 slot, n_ctx_blocks):
    hkv = pl.program_id(1)
    step = pl.program_id(2)
    rows_q = ATTN_GROUP * ATTN_BLOCK
    grp = lax.broadcasted_iota(jnp.int32, (rows_q, 1), 0) // ATTN_BLOCK
    sink = jnp.zeros((rows_q, 1), F32)
    for g in range(ATTN_GROUP):
        sink = jnp.where(grp == g, sink_ref[slot, hkv * ATTN_GROUP + g] * LOG2_E, sink)

    @pl.when(step < n_ctx_blocks)
    def _():
        _attn_block(q_ref, o_ref, sink, (kc_ref[...],), (vc_ref[...],), None)

    @pl.when(step >= n_ctx_blocks)
    def _():
        kb = jnp.concatenate([kp_ref[...], ko_ref[...], kn_ref[...]], axis=0)
        vb = jnp.concatenate([vp_ref[...], vo_ref[...], vn_ref[...]], axis=0)
        _attn_block(q_ref, o_ref, sink, (kc_ref[...], kb), (vc_ref[...], vb), bias_ref[...])


def _band_bias():
    i = np.arange(ATTN_BLOCK)[:, None]
    j = np.arange(3 * ATTN_BLOCK)[None, :]
    window = (j >= i) & (j <= i + 2 * ATTN_BLOCK)
    first = window & (j >= ATTN_BLOCK)
    last = window & (j < 2 * ATTN_BLOCK)
    return jnp.asarray(np.where(np.stack([first, window, last]), 0.0, NEG_INF), dtype=F32)


def _attention(p, sink, slot, rows):
    r = p.shape[0]
    dq = ATTN_HEADS * HEAD_DIM
    gw = ATTN_GROUP * HEAD_DIM
    kcol = dq // HEAD_DIM
    vcol = kcol + ATTN_KV_HEADS
    L, S, B = rows.ctx_len, rows.seq, rows.batch
    nb = S // ATTN_BLOCK
    nbc = L // ATTN_BLOCK
    lat0 = rows.n_ctx // ATTN_BLOCK
    assert nb >= 2

    def q_block(b, s):
        return jnp.where(s < nbc, b * nbc + s, lat0 + b * nb + (s - nbc))

    def band(col0, shift):
        return pl.BlockSpec((ATTN_BLOCK, HEAD_DIM),
                            lambda b, h, s: (lat0 + b * nb + jnp.clip(s - nbc + shift, 0, nb - 1), col0 + h))

    return pl.pallas_call(
        functools.partial(_attn_kernel, slot=slot, n_ctx_blocks=nbc),
        grid=(B, ATTN_KV_HEADS, nbc + nb),
        in_specs=[pl.BlockSpec(memory_space=pltpu.SMEM),
                  pl.BlockSpec((ATTN_BLOCK, gw), lambda b, h, s: (q_block(b, s), h)),
                  pl.BlockSpec((L, HEAD_DIM), lambda b, h, s: (b, kcol + h)),
                  pl.BlockSpec((L, HEAD_DIM), lambda b, h, s: (b, vcol + h)),
                  band(kcol, -1), band(kcol, 0), band(kcol, 1), band(vcol, -1), band(vcol, 0), band(vcol, 1),
                  pl.BlockSpec((None, ATTN_BLOCK, 3 * ATTN_BLOCK),
                               lambda b, h, s: (jnp.where(s <= nbc, 0, jnp.where(s == nbc + nb - 1, 2, 1)), 0, 0))],
        out_specs=pl.BlockSpec((ATTN_BLOCK, gw), lambda b, h, s: (q_block(b, s), h)),
        out_shape=jax.ShapeDtypeStruct((r, dq), BF16),
        compiler_params=_cparams(3),
        name="attention",
    )(sink, p, p, p, p, p, p, p, p, p, _band_bias())


def _dft_cs(n, scale):
    k = np.arange(n)
    ang = 2.0 * np.pi * ((k[:, None] * k[None, :]) % n) / n
    return np.cos(ang) * scale, np.sin(ang) * scale


def _chan_dft_kernel(u_ref, m_ref, o_ref, *, gd):
    for g in range(u_ref.shape[1] // gd):
        v = _dot(u_ref[:, g * gd:(g + 1) * gd], m_ref[...])
        o_ref[0, :, g * gd:(g + 1) * gd] = v[:, :gd].astype(o_ref.dtype)
        o_ref[1, :, g * gd:(g + 1) * gd] = v[:, gd:].astype(o_ref.dtype)


def _chan_dft(u, row0, n_rows, mat):
    d = u.shape[1]
    gd = d // FNET_GROUPS
    t0 = row0 // ROW_TILE
    return pl.pallas_call(
        functools.partial(_chan_dft_kernel, gd=gd),
        grid=(n_rows // ROW_TILE,),
        in_specs=[pl.BlockSpec((ROW_TILE, d), lambda i: (t0 + i, 0)),
                  pl.BlockSpec((gd, 2 * gd), lambda i: (0, 0))],
        out_specs=pl.BlockSpec((2, ROW_TILE, d), lambda i: (0, i, 0)),
        out_shape=jax.ShapeDtypeStruct((2, n_rows, d), BF16),
        compiler_params=_cparams(1),
        name="fnet_chan_dft",
    )(u, mat)


def _seq_dft_a_kernel(x_ref, chan_ref, m_ref, tc_ref, ts_ref, o_ref, *, gd):
    n1 = x_ref.shape[0]
    parts = [_dot(x_ref[:, g * gd:(g + 1) * gd], chan_ref[...]) for g in range(x_ref.shape[1] // gd)]
    vr = jnp.concatenate([p[:, :gd] for p in parts], axis=1).astype(BF16)
    vi = jnp.concatenate([p[:, gd:] for p in parts], axis=1).astype(BF16)
    z = _dot(m_ref[...], jnp.concatenate([vr, vi], axis=0))
    zr, zi = z[:n1], z[n1:]
    tc, ts = tc_ref[...], ts_ref[...]
    for c in range(z.shape[1] // LANES):
        sl = slice(c * LANES, (c + 1) * LANES)
        o_ref[0, :, sl] = (zr[:, sl] * tc + zi[:, sl] * ts).astype(o_ref.dtype)
        o_ref[1, :, sl] = (zi[:, sl] * tc - zr[:, sl] * ts).astype(o_ref.dtype)


def _seq_dft_c_kernel(x_ref, m_ref, o_ref):
    x = jnp.concatenate([x_ref[0], x_ref[1]], axis=0)
    o_ref[...] = _dot(m_ref[...], x).astype(o_ref.dtype)


def _seq_dft_c_blocked_kernel(x_ref, m_ref, o_ref):
    two, n2, kb, d = x_ref.shape
    x = x_ref[...].reshape(two * n2 * kb, d)
    o_ref[...] = _dot(m_ref[...], x).reshape(n2, kb, d).astype(o_ref.dtype)


def _fourier_tokens(u, rows):
    d = u.shape[1]
    gd = d // FNET_GROUPS
    B, S, L = rows.batch, rows.seq, rows.ctx_len
    n2 = GRID_W
    n1 = S // n2
    cc, sc = _dft_cs(gd, gd ** -0.5)
    chan = jnp.asarray(np.concatenate([cc, -sc], axis=1), dtype=BF16)

    x_lat = u[rows.n_ctx:].reshape(B, n1, n2 * d)
    c1, s1 = _dft_cs(n1, n1 ** -0.5)
    m1 = jnp.asarray(np.block([[c1, s1], [-s1, c1]]), dtype=BF16)
    ang = 2.0 * np.pi * (np.arange(n2)[:, None] * np.arange(n1)[None, :]) / S
    tw_c = jnp.asarray(np.broadcast_to(np.cos(ang)[:, :, None], (n2, n1, LANES)), dtype=F32)
    tw_s = jnp.asarray(np.broadcast_to(np.sin(ang)[:, :, None], (n2, n1, LANES)), dtype=F32)
    z = pl.pallas_call(
        functools.partial(_seq_dft_a_kernel, gd=gd),
        grid=(B, n2),
        in_specs=[pl.BlockSpec((None, n1, d), lambda b, t: (b, 0, t)),
                  pl.BlockSpec((gd, 2 * gd), lambda b, t: (0, 0)),
                  pl.BlockSpec((2 * n1, 2 * n1), lambda b, t: (0, 0)),
                  pl.BlockSpec((None, n1, LANES), lambda b, t: (t, 0, 0)),
                  pl.BlockSpec((None, n1, LANES), lambda b, t: (t, 0, 0))],
        out_specs=pl.BlockSpec((2, None, None, n1, d), lambda b, t: (0, b, t, 0, 0)),
        out_shape=jax.ShapeDtypeStruct((2, B, n2, n1, d), BF16),
        compiler_params=_cparams(2),
        name="fnet_seq_dft_a",
    )(x_lat, chan, m1, tw_c, tw_s)
    c2, s2 = _dft_cs(n2, n2 ** -0.5)
    kb = DFT_POS_BLOCK
    m2 = jnp.asarray(np.kron(np.concatenate([c2, s2], axis=1), np.eye(kb)), dtype=BF16)
    y_lat = pl.pallas_call(
        _seq_dft_c_blocked_kernel,
        grid=(B, n1 // kb),
        in_specs=[pl.BlockSpec((2, None, n2, kb, d), lambda b, j: (0, b, 0, j, 0)),
                  pl.BlockSpec((n2 * kb, 2 * n2 * kb), lambda b, j: (0, 0))],
        out_specs=pl.BlockSpec((None, n2, kb, d), lambda b, j: (b, 0, j, 0)),
        out_shape=jax.ShapeDtypeStruct((B, n2, n1, d), BF16),
        compiler_params=_cparams(2),
        name="fnet_seq_dft_c",
    )(z, m2)

    vc = _chan_dft(u, 0, B * L, chan).reshape(2, B, L, d)
    cl, sl = _dft_cs(L, L ** -0.5)
    ml = jnp.asarray(np.concatenate([cl, sl], axis=1), dtype=BF16)
    y_ctx = pl.pallas_call(
        _seq_dft_c_kernel,
        grid=(B, 1),
        in_specs=[pl.BlockSpec((2, None, L, d), lambda b, j: (0, b, 0, 0)),
                  pl.BlockSpec((L, 2 * L), lambda b, j: (0, 0))],
        out_specs=pl.BlockSpec((None, L, d), lambda b, j: (b, 0, 0)),
        out_shape=jax.ShapeDtypeStruct((B, L, d), BF16),
        compiler_params=_cparams(2),
        name="fnet_ctx_dft",
    )(vc, ml)
    return jnp.concatenate([y_ctx.reshape(B * L, d), y_lat.reshape(B * S, d)], axis=0)


def _hgrn_scan_kernel(q_ref, f_ref, v_ref, lb_ref, tri_ref, o_ref, st_ref, *, reverse):
    @pl.when(pl.program_id(1) == 0)
    def _():
        st_ref[...] = jnp.zeros(st_ref.shape, st_ref.dtype)

    c = HGRN_CHUNK
    n_sub = q_ref.shape[0] // c
    lb = lb_ref[...]
    tri = tri_ref[...]
    ti = lax.broadcasted_iota(jnp.int32, (c, c), 0)
    si = lax.broadcasted_iota(jnp.int32, (c, c), 1)
    keep = (si >= ti) if reverse else (si <= ti)
    order = range(n_sub - 1, -1, -1) if reverse else range(n_sub)
    for sub in order:
        rs = slice(sub * c, (sub + 1) * c)
        q = q_ref[rs, :].astype(F32)
        f = lb + (1.0 - lb) * _sigmoid(f_ref[rs, :].astype(F32))
        k = 1.0 - f
        bsum = jnp.dot(tri, jnp.log(f), preferred_element_type=F32, precision=lax.Precision.HIGHEST)
        b_end = bsum[0:1, :] if reverse else bsum[c - 1:c, :]
        decay = jnp.exp(b_end)
        q_in = (q * jnp.exp(bsum)).astype(BF16)
        k_inf = k * jnp.exp(-bsum)
        k_in = k_inf.astype(BF16)
        k_out = (k_inf * decay).astype(BF16)
        v = v_ref[rs, :]
        vt = v.astype(F32)
        for h in range(HGRN_HEADS):
            ks = slice(h * HGRN_DK, (h + 1) * HGRN_DK)
            vs = slice(h * HGRN_DV, (h + 1) * HGRN_DV)
            a = jnp.where(keep, _dot_nt(q_in[:, ks], k_in[:, ks]), 0.0)
            st = st_ref[h]
            o = _dot(a.astype(BF16), v[:, vs]) + _dot_nt(q_in[:, ks], st.astype(BF16))
            o_ref[rs, vs] = o
            st_ref[h] = st * decay[:, ks] + _dot(vt[:, vs].T.astype(BF16), k_out[:, ks])


def _hgrn_scan(pm, lb, rows, reverse):
    r = pm.shape[0]
    hk = HGRN_HEADS * HGRN_DK
    B, S, L = rows.batch, rows.seq, rows.ctx_len
    cs, ls = L // SCAN_ROWS, S // SCAN_ROWS
    direction = 1 if reverse else 0

    def row_block(b, s):
        if reverse:
            ctx = b * cs + (cs - 1 - s)
            lat = B * cs + b * ls + (ls - 1 - (s - cs))
        else:
            ctx = b * cs + s
            lat = B * cs + b * ls + (s - cs)
        return jnp.where(s < cs, ctx, lat)

    c = HGRN_CHUNK
    tri_np = np.triu(np.ones((c, c))) if reverse else np.tril(np.ones((c, c)))
    tri = jnp.asarray(tri_np, dtype=F32)
    col = lambda j: pl.BlockSpec((SCAN_ROWS, hk), lambda b, s: (row_block(b, s), j))
    return pl.pallas_call(
        functools.partial(_hgrn_scan_kernel, reverse=reverse),
        grid=(B, cs + ls),
        in_specs=[col(0), col(1 + direction), col(3),
                  pl.BlockSpec((None, 1, hk), lambda b, s: (direction, 0, 0)),
                  pl.BlockSpec((c, c), lambda b, s: (0, 0))],
        out_specs=pl.BlockSpec((SCAN_ROWS, HGRN_HEADS * HGRN_DV), lambda b, s: (row_block(b, s), 0)),
        out_shape=jax.ShapeDtypeStruct((r, HGRN_HEADS * HGRN_DV), F32),
        scratch_shapes=[pltpu.VMEM((HGRN_HEADS, HGRN_DV, HGRN_DK), F32)],
        compiler_params=_cparams(2),
        name="hgrn_scan_bwd" if reverse else "hgrn_scan_fwd",
    )(pm, pm, pm, lb, tri)


def _hgrn_readout_kernel(of_ref, ob_ref, g_ref, ng_ref, o_ref):
    for h in range(HGRN_HEADS):
        vs = slice(h * HGRN_DV, (h + 1) * HGRN_DV)
        o = of_ref[:, vs] + ob_ref[:, vs]
        o = o * lax.rsqrt(jnp.mean(o * o, axis=-1, keepdims=True) + RMS_EPS) * ng_ref[:, vs]
        o_ref[:, vs] = (o * g_ref[:, vs].astype(F32)).astype(o_ref.dtype)


def _hgrn_readout(o_f, o_b, pm, norm_g, slot):
    r, d = o_f.shape
    gcol = pm.shape[1] // d - 1
    return pl.pallas_call(
        _hgrn_readout_kernel,
        grid=(r // ROW_TILE,),
        in_specs=[pl.BlockSpec((ROW_TILE, d), lambda i: (i, 0)),
                  pl.BlockSpec((ROW_TILE, d), lambda i: (i, 0)),
                  pl.BlockSpec((ROW_TILE, d), lambda i: (i, gcol)),
                  pl.BlockSpec((None, 1, d), lambda i: (slot, 0, 0))],
        out_specs=pl.BlockSpec((ROW_TILE, d), lambda i: (i, 0)),
        out_shape=jax.ShapeDtypeStruct((r, d), BF16),
        compiler_params=_cparams(1),
        name="hgrn_readout",
    )(o_f, o_b, pm, norm_g)


def _moe_plan(route, n_tiles):
    e_flat = jnp.concatenate([route[:, 0], route[:, 1]]).astype(jnp.int32)
    onehot = (e_flat[:, None] == jnp.arange(N_EXPERTS, dtype=jnp.int32)[None, :]).astype(jnp.int32)
    csum = jnp.cumsum(onehot, axis=0)
    counts = csum[-1]
    rank = jnp.sum(csum * onehot, axis=1) - 1
    padded = ((counts + MOE_ROW_TILE - 1) // MOE_ROW_TILE) * MOE_ROW_TILE
    ends = jnp.cumsum(padded)
    starts = ends - padded
    dest = jnp.sum(starts[None, :] * onehot, axis=1) + rank
    n_used = (ends[-1] // MOE_ROW_TILE).astype(jnp.int32)
    tile = jnp.arange(n_tiles, dtype=jnp.int32)
    te = jnp.sum((ends[None, :] <= (tile * MOE_ROW_TILE)[:, None]).astype(jnp.int32), axis=1)
    te = jnp.minimum(te, N_EXPERTS - 1)
    te_last = jnp.sum(jnp.where(tile == n_used - 1, te, 0))
    te = jnp.where(tile < n_used, te, te_last).astype(jnp.int32)
    return dest, te, n_used.reshape(1)


def _moe_gather_kernel(dest_ref, nu_ref, src_ref, o_ref, tok_ref, buf_ref, sem, *, n_tokens, row0):
    i = pl.program_id(0)
    n_used = nu_ref[0]

    def row_copy(slot, r, t):
        return pltpu.make_async_copy(src_ref.at[pl.ds(t, 1), :], buf_ref.at[slot, pl.ds(r, 1), :], sem.at[slot])

    def issue_tile(tile):
        slot = tile % 2
        base = tile * MOE_ROW_TILE

        def issue(r, carry):
            row_copy(slot, r, tok_ref[base + r]).start()
            return carry

        lax.fori_loop(0, MOE_ROW_TILE, issue, 0, unroll=8)

    @pl.when(i == 0)
    def _():
        def fill(p, carry):
            tok_ref[p] = row0
            return carry

        def place(t, carry):
            tok_ref[dest_ref[t]] = row0 + t
            tok_ref[dest_ref[n_tokens + t]] = row0 + t
            return carry

        lax.fori_loop(0, tok_ref.shape[0], fill, 0, unroll=8)
        lax.fori_loop(0, n_tokens, place, 0, unroll=8)
        issue_tile(i)

    @pl.when(i + 1 < n_used)
    def _():
        issue_tile(i + 1)

    @pl.when(i < n_used)
    def _():
        slot = i % 2

        def drain(r, carry):
            row_copy(slot, r, 0).wait()
            return carry

        lax.fori_loop(0, MOE_ROW_TILE, drain, 0, unroll=8)
        o_ref[...] = _unpack_bf16_pairs(buf_ref[slot]).astype(o_ref.dtype)

    @pl.when(i >= n_used)
    def _():
        o_ref[...] = jnp.zeros(o_ref.shape, o_ref.dtype)


def _moe_gather(dest, n_used, u, n_tiles, row0):
    d = 2 * u.shape[1]
    p_rows = n_tiles * MOE_ROW_TILE
    return pl.pallas_call(
        functools.partial(_moe_gather_kernel, n_tokens=dest.shape[0] // 2, row0=row0),
        grid_spec=pltpu.PrefetchScalarGridSpec(
            num_scalar_prefetch=2,
            grid=(n_tiles,),
            in_specs=[pl.BlockSpec(memory_space=pl.ANY)],
            out_specs=pl.BlockSpec((MOE_ROW_TILE, d), lambda i, dst, nu: (i, 0)),
            scratch_shapes=[pltpu.SMEM((p_rows,), jnp.int32),
                            pltpu.VMEM((2, MOE_ROW_TILE, d // 2), jnp.uint32), pltpu.SemaphoreType.DMA((2,))]),
        out_shape=jax.ShapeDtypeStruct((p_rows, d), BF16),
        compiler_params=_cparams(1),
        name="moe_gather",
    )(dest, n_used, u)


def _expert_changed(te_ref):
    i = pl.program_id(1)
    return (i == 0) | (te_ref[i] != te_ref[jnp.maximum(i - 1, 0)])


def _gmm_swiglu_kernel(te_ref, nu_ref, x_ref, wa_ref, wb_ref, w2_ref, o_ref, w2bf_ref, wbfa_ref, wbfb_ref):
    i = pl.program_id(1)

    @pl.when(_expert_changed(te_ref))
    def _():
        wbfa_ref[...] = wa_ref[...].astype(BF16)
        wbfb_ref[...] = wb_ref[...].astype(BF16)

    @pl.when(i < nu_ref[0])
    def _():
        x = x_ref[...]
        o_ref[...] = (_silu(_dot(x, wbfa_ref[...])) * _dot(x, wbfb_ref[...])).astype(o_ref.dtype)

    @pl.when(i >= nu_ref[0])
    def _():
        o_ref[...] = jnp.zeros(o_ref.shape, o_ref.dtype)

    w2bf_ref[...] = w2_ref[...].astype(BF16)


def _gmm_down_kernel(te_ref, nu_ref, x_ref, w_ref, o_ref):
    i = pl.program_id(1)

    @pl.when(i < nu_ref[0])
    def _():
        o_ref[...] = _pack_bf16_pairs(_dot(x_ref[...], w_ref[...]))

    @pl.when(i >= nu_ref[0])
    def _():
        o_ref[...] = jnp.zeros(o_ref.shape, o_ref.dtype)


def _moe_experts(xg, te, n_used, w13, w2, layer, n_tiles):
    p_rows, d = xg.shape
    n_layers, n_exp, f, d_out = w2.shape
    tn = COL_TILE
    nf = f // tn
    used = lambda i, nu: jnp.minimum(i, nu[0] - 1)
    ch = _side_cast_rows(n_exp * f, nf * n_tiles)
    n_chunks = n_exp * f // ch
    chunk = lambda n, i: jnp.minimum(n * n_tiles + i, n_chunks - 1)
    hidden, w2_bf16 = pl.pallas_call(
        _gmm_swiglu_kernel,
        grid_spec=pltpu.PrefetchScalarGridSpec(
            num_scalar_prefetch=2,
            grid=(nf, n_tiles),
            in_specs=[pl.BlockSpec((MOE_ROW_TILE, d), lambda n, i, te, nu: (used(i, nu), 0)),
                      pl.BlockSpec((None, None, d, tn), lambda n, i, te, nu: (layer, te[i], 0, n)),
                      pl.BlockSpec((None, None, d, tn), lambda n, i, te, nu: (layer, te[i], 0, n + nf)),
                      pl.BlockSpec((None, ch, d_out), lambda n, i, te, nu: (layer, chunk(n, i), 0))],
            out_specs=[pl.BlockSpec((MOE_ROW_TILE, tn), lambda n, i, te, nu: (i, n)),
                       pl.BlockSpec((ch, d_out), lambda n, i, te, nu: (chunk(n, i), 0))],
            scratch_shapes=[pltpu.VMEM((d, tn), BF16), pltpu.VMEM((d, tn), BF16)]),
        out_shape=[jax.ShapeDtypeStruct((p_rows, f), BF16), jax.ShapeDtypeStruct((n_exp * f, d_out), BF16)],
        compiler_params=_cparams(2),
        name="moe_w13",
    )(te, n_used, xg, w13, w13, w2.reshape(n_layers, n_exp * f, d_out))
    tn2 = MOE_W2_COL_TILE
    return pl.pallas_call(
        _gmm_down_kernel,
        grid_spec=pltpu.PrefetchScalarGridSpec(
            num_scalar_prefetch=2,
            grid=(d_out // tn2, n_tiles),
            in_specs=[pl.BlockSpec((MOE_ROW_TILE, f), lambda n, i, te, nu: (used(i, nu), 0)),
                      pl.BlockSpec((None, f, tn2), lambda n, i, te, nu: (te[i], 0, n))],
            out_specs=pl.BlockSpec((MOE_ROW_TILE, tn2 // 2), lambda n, i, te, nu: (i, n)),
            scratch_shapes=[]),
        out_shape=jax.ShapeDtypeStruct((p_rows, d_out // 2), jnp.uint32),
        compiler_params=_cparams(2),
        name="moe_w2",
    )(te, n_used, hidden, w2_bf16.reshape(n_exp, f, d_out))


def _moe_combine_kernel(p1_ref, p2_ref, y_ref, h_ref, g_ref, route_ref, *rest, mode):
    if mode == "final":
        ng_ref, o_ref, buf_ref, sem = rest
    else:
        ng_ref, sh_ref, sc_ref, o_ref, u_ref, buf_ref, sem = rest
    i = pl.program_id(0)

    def row_copy(slot, k, r, p):
        return pltpu.make_async_copy(y_ref.at[pl.ds(p, 1), :], buf_ref.at[slot, k, pl.ds(r, 1), :], sem.at[slot])

    def issue_tile(tile):
        slot = tile % 2
        base = tile * COMBINE_ROW_TILE

        def issue(r, carry):
            row_copy(slot, 0, r, p1_ref[base + r]).start()
            row_copy(slot, 1, r, p2_ref[base + r]).start()
            return carry

        lax.fori_loop(0, COMBINE_ROW_TILE, issue, 0, unroll=8)

    @pl.when(i == 0)
    def _():
        issue_tile(i)

    @pl.when(i + 1 < pl.num_programs(0))
    def _():
        issue_tile(i + 1)

    slot = i % 2

    def drain(r, carry):
        row_copy(slot, 0, r, 0).wait()
        row_copy(slot, 1, r, 0).wait()
        return carry

    lax.fori_loop(0, COMBINE_ROW_TILE, drain, 0, unroll=8)
    w1 = route_ref[:, 2:3]
    w2 = route_ref[:, 3:4]
    half = MOE_W2_COL_TILE // 2
    y = jnp.concatenate(
        [w1 * _unpack_bf16_pairs(buf_ref[slot, 0, :, c * half:(c + 1) * half])
         + w2 * _unpack_bf16_pairs(buf_ref[slot, 1, :, c * half:(c + 1) * half])
         for c in range(buf_ref.shape[3] // half)], axis=1)
    h_new = h_ref[...] + g_ref[...] * y
    if mode == "final":
        o_ref[...] = _rms_norm(h_new, ng_ref[...])
    else:
        o_ref[...] = h_new
        u_ref[...] = (_rms_norm(h_new, ng_ref[...]) * (1.0 + sc_ref[...]) + sh_ref[...]).astype(u_ref.dtype)


def _moe_combine(pos1, pos2, y, h, mods, route, rows, mod_layer, row0, post):
    d = h.shape[1]
    r = h.shape[0] - row0
    g = rows.groups
    tr = COMBINE_ROW_TILE
    t0 = row0 // tr
    mod = lambda layer, which: pl.BlockSpec(
        (None, 1, d), lambda i, p1, p2: ((layer * 6 + which) * g + rows.group(t0 + i, tr), 0, 0))
    tile = pl.BlockSpec((tr, d), lambda i, p1, p2: (i, 0))
    in_specs = [pl.BlockSpec(memory_space=pl.ANY),
                pl.BlockSpec((tr, d), lambda i, p1, p2: (t0 + i, 0)),
                mod(mod_layer, 5),
                pl.BlockSpec((tr, LANES), lambda i, p1, p2: (t0 + i, 0))]
    if post[0] == "final":
        in_specs.append(pl.BlockSpec((1, d), lambda i, p1, p2: (0, 0)))
        extra = (post[1],)
        out_specs, out_shape = tile, jax.ShapeDtypeStruct((r, d), F32)
    else:
        _, norm_w, nxt = post
        in_specs += [pl.BlockSpec((None, 1, d), lambda i, p1, p2: (nxt, 0, 0)), mod(nxt, 0), mod(nxt, 1)]
        extra = (norm_w, mods, mods)
        out_specs = [tile, tile]
        out_shape = [jax.ShapeDtypeStruct((r, d), F32), jax.ShapeDtypeStruct((r, d), BF16)]
    return pl.pallas_call(
        functools.partial(_moe_combine_kernel, mode=post[0]),
        grid_spec=pltpu.PrefetchScalarGridSpec(
            num_scalar_prefetch=2,
            grid=(r // tr,),
            in_specs=in_specs,
            out_specs=out_specs,
            scratch_shapes=[pltpu.VMEM((2, 2, tr, d // 2), jnp.uint32), pltpu.SemaphoreType.DMA((2,))]),
        out_shape=out_shape,
        compiler_params=_cparams(1),
        name="moe_combine",
    )(pos1, pos2, y, h, mods, route, *extra)


def _moe_ffn(h, u, route, mods, rows, layer, w13, w2, moe_layer, row0, post):
    r = h.shape[0] - row0
    n_tiles = -(-(2 * r + N_EXPERTS * (MOE_ROW_TILE - 1)) // MOE_ROW_TILE)
    dest, te, n_used = _moe_plan(route[row0:], n_tiles)
    xg = _moe_gather(dest, n_used, u, n_tiles, row0)
    y = _moe_experts(xg, te, n_used, w13, w2, moe_layer, n_tiles)
    return _moe_combine(dest[:r], dest[r:], y, h, mods, route, rows, layer, row0, post)


def _hgrn_lower_bounds(lb_logits, layer):
    gamma = jax.nn.softmax(lb_logits.astype(F32), axis=0)
    lb = jnp.cumsum(gamma, axis=0) - gamma[0]
    return lb[layer]


def kernel(x, c, ctx, c_ctx, ada_w, ada_b, norm_mix, norm_ffn, norm_final, attn_wqkv, attn_wo, attn_sink, fnet_wo,
           hgrn_win, hgrn_lb, hgrn_norm, hgrn_wo, ffn_w13, ffn_w2, moe_router, moe_w13, moe_w2):
    B, S, D = x.shape
    L = ctx.shape[1]
    depth = ada_w.shape[0]
    rows = _Rows(B, S, L)
    G = rows.groups

    cond = jnp.concatenate([c_ctx[None, :], c], axis=0)
    cond = jnp.pad(_silu(cond), ((0, 16 - G), (0, 0))).astype(BF16)
    mods = _ada_mods(cond, ada_w, ada_b)
    mods = mods[:, :G, :].reshape(depth, G, 6, D).transpose(0, 2, 1, 3).reshape(depth * 6 * G, 1, D)

    h = jnp.concatenate([ctx.reshape(B * L, D), x.reshape(B * S, D)], axis=0)
    norm_mix3 = norm_mix.reshape(depth, 1, D)
    norm_ffn3 = norm_ffn.reshape(depth, 1, D)
    rope_tables = _rope_tables(rows)
    router_pad = jnp.pad(moe_router, ((0, 0), (0, 0), (0, LANES - moe_router.shape[-1]))).astype(BF16)

    u = None
    for i in range(depth):
        kind, slot = i % N_MIXERS, i // N_MIXERS
        j = i // 2
        dense = i % 2 == 0
        if u is None:
            u = _norm_mod(h, norm_mix3, mods, rows, i, 0)
        if kind == 0:
            p = _mm_qkv_rope(u, attn_wqkv, slot, rope_tables, rows)
            o, w_out, name = _attention(p, attn_sink, slot, rows), attn_wo, "attn_out"
        elif kind == 1:
            o, w_out, name = _fourier_tokens(u, rows), fnet_wo, "fnet_out"
        else:
            lb = _hgrn_lower_bounds(hgrn_lb, i).reshape(2, 1, HGRN_HEADS * HGRN_DK)
            pm = _mm_hgrn_in(u, hgrn_win, slot)
            o_f = _hgrn_scan(pm, lb, rows, reverse=False)
            o_b = _hgrn_scan(pm, lb, rows, reverse=True)
            o, w_out, name = _hgrn_readout(o_f, o_b, pm, hgrn_norm.reshape(-1, 1, D), slot), hgrn_wo, "hgrn_out"
        u = None

        if dense:
            h, v = _mixer_out(o, w_out, slot, h, mods, rows, i, norm_ffn3, name=name)
            g, w2_bf16 = _mm_swiglu(v, ffn_w13, ffn_w2, j)
            h = _mm_resid_bf16w(g, w2_bf16, h, mods, rows, i, 5, FFN_W2_COL_TILE, name="ffn_out")
            continue
        h, v, route = _mixer_out(o, w_out, slot, h, mods, rows, i, norm_ffn3, router_pad, j, name=name)
        if i == depth - 1:
            out = _moe_ffn(h, v, route, mods, rows, i, moe_w13, moe_w2, j, rows.n_ctx,
                           ("final", norm_final.reshape(1, D)))
            return out.reshape(B, S, D)
        h, u = _moe_ffn(h, v, route, mods, rows, i, moe_w13, moe_w2, j, 0, ("next", norm_mix3, i + 1))

    return _final_norm(h, norm_final, rows).reshape(B, S, D)
```

```python
import functools

import numpy as np
import jax
import jax.numpy as jnp
from jax import lax
from jax.experimental import pallas as pl
from jax.experimental.pallas import tpu as pltpu

F32 = jnp.float32
BF16 = jnp.bfloat16

N_MIXERS = 3
RMS_EPS = 1e-6
NEG_INF = -1e30
LOG2_E = 1.4426950408889634
GRID_W = 64
ATTN_HEADS = 16
ATTN_KV_HEADS = 4
ATTN_GROUP = ATTN_HEADS // ATTN_KV_HEADS
HEAD_DIM = 128
ATTN_BLOCK = 128
ROPE_THETA = 10000.0
FNET_GROUPS = 8
HGRN_HEADS = 16
HGRN_DK = 128
HGRN_DV = 128
HGRN_CHUNK = 64
N_EXPERTS = 8

LANES = 128
ROW_TILE = 512
COL_TILE = 1024
SWIGLU_COL_TILE = 512
FFN_W2_COL_TILE = 1024
MOE_W2_COL_TILE = 1024
MOE_ROW_TILE = 512
COMBINE_ROW_TILE = 256
SCAN_ROWS = 256
DFT_POS_BLOCK = 16
VMEM_LIMIT_BYTES = 56 * 1024 * 1024


def _cparams(n_axes):
    return pltpu.CompilerParams(dimension_semantics=("arbitrary",) * n_axes,
                                vmem_limit_bytes=VMEM_LIMIT_BYTES)


def _dot(a, b):
    return jnp.dot(a, b, preferred_element_type=F32)


def _dot_nt(a, b):
    return lax.dot_general(a, b, (((1,), (1,)), ((), ())), preferred_element_type=F32)


def _sigmoid(x):
    return 1.0 / (1.0 + jnp.exp(-x))


def _silu(x):
    return x * _sigmoid(x)


def _ada_kernel(a_ref, w_ref, b_ref, o_ref):
    o_ref[...] = _dot(a_ref[...], w_ref[...].astype(BF16)) + b_ref[...]


def _ada_mods(cond_rows, ada_w, ada_b):
    depth, d, n6 = ada_w.shape
    rows = cond_rows.shape[0]
    tn = 1024
    return pl.pallas_call(
        _ada_kernel,
        grid=(depth, n6 // tn),
        in_specs=[pl.BlockSpec((rows, d), lambda l, n: (0, 0)),
                  pl.BlockSpec((None, d, tn), lambda l, n: (l, 0, n)),
                  pl.BlockSpec((None, 1, tn), lambda l, n: (l, 0, n))],
        out_specs=pl.BlockSpec((None, rows, tn), lambda l, n: (l, 0, n)),
        out_shape=jax.ShapeDtypeStruct((depth, rows, n6), F32),
        compiler_params=_cparams(2),
        name="ada_mods",
    )(cond_rows, ada_w, ada_b.reshape(depth, 1, n6))


def _rms_norm(h, gain):
    return h * lax.rsqrt(jnp.mean(h * h, axis=-1, keepdims=True) + RMS_EPS) * gain


def _norm_mod_kernel(h_ref, g_ref, sh_ref, sc_ref, o_ref):
    o_ref[...] = (_rms_norm(h_ref[...], g_ref[...]) * (1.0 + sc_ref[...]) + sh_ref[...]).astype(o_ref.dtype)


def _input_norm_kernel(ctx_ref, x_ref, g_ref, sh_ref, sc_ref, h_ref, u_ref, *, ctx_tiles):
    def emit(src_ref):
        h = src_ref[...]
        h_ref[...] = h
        u_ref[...] = (_rms_norm(h, g_ref[...]) * (1.0 + sc_ref[...]) + sh_ref[...]).astype(u_ref.dtype)

    @pl.when(pl.program_id(0) < ctx_tiles)
    def _():
        emit(ctx_ref)

    @pl.when(pl.program_id(0) >= ctx_tiles)
    def _():
        emit(x_ref)


def _top2_route(u, router):
    logits = _dot(u.astype(BF16), router)
    lane = lax.broadcasted_iota(jnp.int32, logits.shape, 1)
    l1 = jnp.where(lane < N_EXPERTS, logits, NEG_INF)
    m1 = jnp.max(l1, axis=-1, keepdims=True)
    i1 = jnp.min(jnp.where(l1 == m1, lane, LANES), axis=-1, keepdims=True)
    l2 = jnp.where(lane == i1, NEG_INF, l1)
    m2 = jnp.max(l2, axis=-1, keepdims=True)
    i2 = jnp.min(jnp.where(l2 == m2, lane, LANES), axis=-1, keepdims=True)
    e2 = jnp.exp(m2 - m1)
    w1 = 1.0 / (1.0 + e2)
    w2 = e2 * w1
    return jnp.where(lane == 0, i1.astype(F32),
                     jnp.where(lane == 1, i2.astype(F32), jnp.where(lane == 2, w1, jnp.where(lane == 3, w2, 0.0))))


def _final_norm_kernel(h_ref, g_ref, o_ref):
    o_ref[...] = _rms_norm(h_ref[...], g_ref[...])


def _cast_weight(w_ref, wbf_ref):
    @pl.when(pl.program_id(1) == 0)
    def _():
        wbf_ref[...] = w_ref[...].astype(BF16)


def _mm_rope_kernel(a_ref, w_ref, cos_ref, sa_ref, sb_ref, o_ref, wbf_ref, *, n_col_tiles, n_q_heads, n_rope_heads,
                    q_scale):
    _cast_weight(w_ref, wbf_ref)
    acc = _dot(a_ref[...], wbf_ref[...])
    heads_per_tile = acc.shape[1] // HEAD_DIM
    for tile in range(n_col_tiles):
        @pl.when(pl.program_id(0) == tile)
        def _(tile=tile):
            for c in range(heads_per_tile):
                head = tile * heads_per_tile + c
                t = acc[:, c * HEAD_DIM:(c + 1) * HEAD_DIM]
                if head < n_rope_heads:
                    t = t * cos_ref[...] + pltpu.roll(t, HEAD_DIM - 32, 1) * sa_ref[...] + pltpu.roll(t, 32, 1) * sb_ref[...]
                if head < n_q_heads:
                    t = t * q_scale
                o_ref[:, c * HEAD_DIM:(c + 1) * HEAD_DIM] = t.astype(o_ref.dtype)


def _mm_swiglu_kernel(a_ref, wa_ref, wb_ref, w2_ref, o_ref, w2bf_ref, wbfa_ref, wbfb_ref):
    _cast_weight(wa_ref, wbfa_ref)
    _cast_weight(wb_ref, wbfb_ref)
    a = a_ref[...]
    ga = _dot(a, wbfa_ref[...])
    gb = _dot(a, wbfb_ref[...])
    o_ref[...] = (_silu(ga) * gb).astype(o_ref.dtype)
    w2bf_ref[...] = w2_ref[...].astype(BF16)


def _mm_hgrn_in_kernel(a_ref, w_ref, o_ref, wbf_ref, *, tiles_per_segment):
    _cast_weight(w_ref, wbf_ref)
    seg = pl.program_id(0) // tiles_per_segment
    acc = _dot(a_ref[...], wbf_ref[...])

    @pl.when(seg == 0)
    def _():
        o_ref[...] = _silu(acc).astype(o_ref.dtype)

    @pl.when(seg == 4)
    def _():
        o_ref[...] = _sigmoid(acc).astype(o_ref.dtype)

    @pl.when((seg != 0) & (seg != 4))
    def _():
        o_ref[...] = acc.astype(o_ref.dtype)


def _mm_resid_bf16w_kernel(a_ref, w_ref, h_ref, g_ref, o_ref):
    o_ref[...] = h_ref[...] + g_ref[...] * _dot(a_ref[...], w_ref[...])


def _side_cast_rows(total_rows, n_steps):
    for rows in range(16, total_rows + 1, 16):
        if total_rows % rows == 0 and total_rows // rows <= n_steps:
            return rows
    raise ValueError((total_rows, n_steps))


class _Rows:
    def __init__(self, batch, seq, ctx_len):
        self.batch, self.seq, self.ctx_len = batch, seq, ctx_len
        self.n_ctx = batch * ctx_len
        self.n_rows = self.n_ctx + batch * seq
        self.groups = 1 + batch
        assert self.n_ctx % ROW_TILE == 0 and seq % ROW_TILE == 0, (batch, seq, ctx_len)
        assert ctx_len % SCAN_ROWS == 0 and seq % SCAN_ROWS == 0 and ctx_len % ATTN_BLOCK == 0

    def group(self, tile, tile_rows=ROW_TILE):
        ctx_tiles = self.n_ctx // tile_rows
        per_batch = self.seq // tile_rows
        return jnp.where(tile < ctx_tiles, 0, 1 + (tile - ctx_tiles) // per_batch)


def _mod_spec(rows, layer, which, width, col_of, tile_rows=ROW_TILE, tile0=0):
    g = rows.groups
    return pl.BlockSpec((None, 1, width),
                        lambda n, i: ((layer * 6 + which) * g + rows.group(tile0 + i, tile_rows), 0, col_of(n)))


def _mm_qkv_rope(a, w, layer, tables, rows):
    m, k = a.shape
    n_total = w.shape[-1]
    tn = COL_TILE
    ctx_tiles = rows.n_ctx // ROW_TILE
    seq_tiles = rows.seq // ROW_TILE
    tab = lambda n, i: (jnp.where(i < ctx_tiles, i, ctx_tiles + (i - ctx_tiles) % seq_tiles), 0)
    kern = functools.partial(_mm_rope_kernel, n_col_tiles=n_total // tn, n_q_heads=ATTN_HEADS,
                             n_rope_heads=ATTN_HEADS + ATTN_KV_HEADS, q_scale=LOG2_E * HEAD_DIM ** -0.5)
    return pl.pallas_call(
        kern,
        grid=(n_total // tn, m // ROW_TILE),
        in_specs=[pl.BlockSpec((ROW_TILE, k), lambda n, i: (i, 0)),
                  pl.BlockSpec((None, k, tn), lambda n, i: (layer, 0, n)),
                  pl.BlockSpec((ROW_TILE, HEAD_DIM), tab),
                  pl.BlockSpec((ROW_TILE, HEAD_DIM), tab),
                  pl.BlockSpec((ROW_TILE, HEAD_DIM), tab)],
        out_specs=pl.BlockSpec((ROW_TILE, tn), lambda n, i: (i, n)),
        out_shape=jax.ShapeDtypeStruct((m, n_total), BF16),
        scratch_shapes=[pltpu.VMEM((k, tn), BF16)],
        compiler_params=_cparams(2),
        name="mm_qkv_rope",
    )(a, w, *tables)


def _mm_swiglu(a, w13, w2, layer):
    m, k = a.shape
    f = w13.shape[-1] // 2
    d_out = w2.shape[-1]
    tn = SWIGLU_COL_TILE
    nf = f // tn
    n_row_tiles = m // ROW_TILE
    ch = _side_cast_rows(f, nf * n_row_tiles)
    n_chunks = f // ch
    chunk = lambda n, i: jnp.minimum(n * n_row_tiles + i, n_chunks - 1)
    return pl.pallas_call(
        _mm_swiglu_kernel,
        grid=(nf, n_row_tiles),
        in_specs=[pl.BlockSpec((ROW_TILE, k), lambda n, i: (i, 0)),
                  pl.BlockSpec((None, k, tn), lambda n, i: (layer, 0, n)),
                  pl.BlockSpec((None, k, tn), lambda n, i: (layer, 0, n + nf)),
                  pl.BlockSpec((None, ch, d_out), lambda n, i: (layer, chunk(n, i), 0))],
        out_specs=[pl.BlockSpec((ROW_TILE, tn), lambda n, i: (i, n)),
                   pl.BlockSpec((ch, d_out), lambda n, i: (chunk(n, i), 0))],
        out_shape=[jax.ShapeDtypeStruct((m, f), BF16), jax.ShapeDtypeStruct((f, d_out), BF16)],
        scratch_shapes=[pltpu.VMEM((k, tn), BF16), pltpu.VMEM((k, tn), BF16)],
        compiler_params=_cparams(2),
        name="mm_swiglu",
    )(a, w13, w13, w2)


def _mm_hgrn_in(a, w, layer):
    m, k = a.shape
    n_total = w.shape[-1]
    tn = COL_TILE
    seg = HGRN_HEADS * HGRN_DK
    assert n_total == 5 * seg and seg % tn == 0
    return pl.pallas_call(
        functools.partial(_mm_hgrn_in_kernel, tiles_per_segment=seg // tn),
        grid=(n_total // tn, m // ROW_TILE),
        in_specs=[pl.BlockSpec((ROW_TILE, k), lambda n, i: (i, 0)),
                  pl.BlockSpec((None, k, tn), lambda n, i: (layer, 0, n))],
        out_specs=pl.BlockSpec((ROW_TILE, tn), lambda n, i: (i, n)),
        out_shape=jax.ShapeDtypeStruct((m, n_total), BF16),
        scratch_shapes=[pltpu.VMEM((k, tn), BF16)],
        compiler_params=_cparams(2),
        name="hgrn_in",
    )(a, w)


def _mm_resid_bf16w(a, w_bf16, h, mods, rows, mod_layer, which, tn, name):
    m, k = a.shape
    n_total = w_bf16.shape[-1]
    return pl.pallas_call(
        _mm_resid_bf16w_kernel,
        grid=(n_total // tn, m // ROW_TILE),
        in_specs=[pl.BlockSpec((ROW_TILE, k), lambda n, i: (i, 0)),
                  pl.BlockSpec((k, tn), lambda n, i: (0, n)),
                  pl.BlockSpec((ROW_TILE, tn), lambda n, i: (i, n)),
                  _mod_spec(rows, mod_layer, which, tn, lambda n: n)],
        out_specs=pl.BlockSpec((ROW_TILE, tn), lambda n, i: (i, n)),
        out_shape=jax.ShapeDtypeStruct((m, n_total), F32),
        compiler_params=_cparams(2),
        name=name,
    )(a, w_bf16, h, mods)


def _cast_bf16_kernel(w_ref, o_ref):
    o_ref[...] = w_ref[...].astype(BF16)


def _cast_bf16(w, layer):
    _, k, n = w.shape
    rows_per_step = 256
    return pl.pallas_call(
        _cast_bf16_kernel,
        grid=(k // rows_per_step,),
        in_specs=[pl.BlockSpec((None, rows_per_step, n), lambda i: (layer, i, 0))],
        out_specs=pl.BlockSpec((rows_per_step, n), lambda i: (i, 0)),
        out_shape=jax.ShapeDtypeStruct((k, n), BF16),
        compiler_params=_cparams(1),
        name="cast_bf16",
    )(w)


def _mixer_out_kernel(a_ref, w_ref, h_ref, g_ref, ng_ref, sh_ref, sc_ref, *rest, route):
    if route:
        r_ref, o_ref, u_ref, route_ref = rest
    else:
        o_ref, u_ref = rest
    h_new = h_ref[...] + g_ref[...] * _dot(a_ref[...], w_ref[...])
    o_ref[...] = h_new
    u = _rms_norm(h_new, ng_ref[...]) * (1.0 + sc_ref[...]) + sh_ref[...]
    u_ref[...] = u.astype(u_ref.dtype)
    if route:
        route_ref[...] = _top2_route(u, r_ref[...])


def _mixer_out(a, w, slot, h, mods, rows, layer, norm_ffn, router=None, moe_layer=0, name="mixer_out"):
    m, k = a.shape
    d = w.shape[-1]
    g = rows.groups
    w_bf16 = _cast_bf16(w, slot)
    mod = lambda which: pl.BlockSpec((None, 1, d), lambda i: ((layer * 6 + which) * g + rows.group(i), 0, 0))
    tile = pl.BlockSpec((ROW_TILE, d), lambda i: (i, 0))
    in_specs = [pl.BlockSpec((ROW_TILE, k), lambda i: (i, 0)),
                pl.BlockSpec((k, d), lambda i: (0, 0)),
                tile, mod(2),
                pl.BlockSpec((None, 1, d), lambda i: (layer, 0, 0)), mod(3), mod(4)]
    args = [a, w_bf16, h, mods, norm_ffn, mods, mods]
    out_specs = [tile, tile]
    if router is None:
        out_shape = [jax.ShapeDtypeStruct((m, d), F32), jax.ShapeDtypeStruct((m, d), BF16)]
    else:
        in_specs.append(pl.BlockSpec((None, d, LANES), lambda i: (moe_layer, 0, 0)))
        args.append(router)
        out_specs.append(pl.BlockSpec((ROW_TILE, LANES), lambda i: (i, 0)))
        out_shape = [jax.ShapeDtypeStruct((m, d), F32), jax.ShapeDtypeStruct((m, d), F32),
                     jax.ShapeDtypeStruct((m, LANES), F32)]
    return pl.pallas_call(
        functools.partial(_mixer_out_kernel, route=router is not None),
        grid=(m // ROW_TILE,),
        in_specs=in_specs,
        out_specs=out_specs,
        out_shape=out_shape,
        compiler_params=_cparams(1),
        name=name,
    )(*args)


def _input_norm(ctx2d, x2d, norm_w, mods, rows):
    d = x2d.shape[1]
    g = rows.groups
    ctx_tiles = rows.n_ctx // ROW_TILE
    mod = lambda which: pl.BlockSpec((None, 1, d), lambda i: (which * g + rows.group(i), 0, 0))
    tile = pl.BlockSpec((ROW_TILE, d), lambda i: (i, 0))
    return pl.pallas_call(
        functools.partial(_input_norm_kernel, ctx_tiles=ctx_tiles),
        grid=(rows.n_rows // ROW_TILE,),
        in_specs=[pl.BlockSpec((ROW_TILE, d), lambda i: (jnp.minimum(i, ctx_tiles - 1), 0)),
                  pl.BlockSpec((ROW_TILE, d), lambda i: (jnp.maximum(i - ctx_tiles, 0), 0)),
                  pl.BlockSpec((None, 1, d), lambda i: (0, 0, 0)), mod(0), mod(1)],
        out_specs=[tile, tile],
        out_shape=[jax.ShapeDtypeStruct((rows.n_rows, d), F32), jax.ShapeDtypeStruct((rows.n_rows, d), BF16)],
        compiler_params=_cparams(1),
        name="input_norm",
    )(ctx2d, x2d, norm_w, mods, mods)


def _norm_mod(h, norm_w, mods, rows, layer, which_shift):
    m, d = h.shape
    return pl.pallas_call(
        _norm_mod_kernel,
        grid=(1, m // ROW_TILE),
        in_specs=[pl.BlockSpec((ROW_TILE, d), lambda n, i: (i, 0)),
                  pl.BlockSpec((None, 1, d), lambda n, i: (layer, 0, 0)),
                  _mod_spec(rows, layer, which_shift, d, lambda n: 0),
                  _mod_spec(rows, layer, which_shift + 1, d, lambda n: 0)],
        out_specs=pl.BlockSpec((ROW_TILE, d), lambda n, i: (i, 0)),
        out_shape=jax.ShapeDtypeStruct((m, d), BF16),
        compiler_params=_cparams(2),
        name="norm_mod",
    )(h, norm_w, mods, mods)


def _final_norm(h, norm_w, rows):
    d = h.shape[1]
    first = rows.n_ctx // ROW_TILE
    n_lat = rows.batch * rows.seq
    return pl.pallas_call(
        _final_norm_kernel,
        grid=(n_lat // ROW_TILE,),
        in_specs=[pl.BlockSpec((ROW_TILE, d), lambda i: (first + i, 0)),
                  pl.BlockSpec((1, d), lambda i: (0, 0))],
        out_specs=pl.BlockSpec((ROW_TILE, d), lambda i: (i, 0)),
        out_shape=jax.ShapeDtypeStruct((n_lat, d), F32),
        compiler_params=_cparams(1),
        name="final_norm",
    )(h, norm_w.reshape(1, d))


def _rope_tables(rows):
    s = rows.seq
    pos = np.arange(s)
    row = (pos // GRID_W).astype(np.float64)
    col = (pos % GRID_W).astype(np.float64)
    sec = HEAD_DIM // 2
    inv = ROPE_THETA ** (-np.arange(0, sec, 2, dtype=np.float64) / sec)
    inv = inv.astype(np.float32).astype(np.float64)
    ang = np.concatenate([row[:, None] * inv, row[:, None] * inv, col[:, None] * inv, col[:, None] * inv], axis=1)
    ang = ang.astype(np.float32).astype(np.float64)
    cos, sin = np.cos(ang), np.sin(ang)
    first_half = (np.arange(HEAD_DIM) % sec) < (sec // 2)
    sa = np.where(first_half[None, :], -sin, 0.0)
    sb = np.where(first_half[None, :], 0.0, sin)
    ident = np.zeros((rows.n_ctx, HEAD_DIM))
    mk = lambda ctx_rows, lat: jnp.asarray(np.concatenate([ctx_rows, lat], axis=0), dtype=F32)
    return mk(ident + 1.0, cos), mk(ident, sa), mk(ident, sb)


def _attn_block(q_ref, o_ref, sink, s_keys, v_keys, bias):
    q = jnp.concatenate([q_ref[:, g * HEAD_DIM:(g + 1) * HEAD_DIM] for g in range(ATTN_GROUP)], axis=0)
    s_c = _dot_nt(q, s_keys[0])
    m = jnp.maximum(jnp.max(s_c, axis=-1, keepdims=True), sink)
    if bias is not None:
        s_b = _dot_nt(q, s_keys[1]) + jnp.concatenate([bias] * ATTN_GROUP, axis=0)
        m = jnp.maximum(m, jnp.max(s_b, axis=-1, keepdims=True))
    p_c = jnp.exp2(s_c - m)
    den = jnp.sum(p_c, axis=-1, keepdims=True) + jnp.exp2(sink - m)
    o = _dot(p_c.astype(BF16), v_keys[0])
    if bias is not None:
        p_b = jnp.exp2(s_b - m)
        den = den + jnp.sum(p_b, axis=-1, keepdims=True)
        o = o + _dot(p_b.astype(BF16), v_keys[1])
    o = o * (1.0 / den)
    for g in range(ATTN_GROUP):
        o_ref[:, g * HEAD_DIM:(g + 1) * HEAD_DIM] = o[g * ATTN_BLOCK:(g + 1) * ATTN_BLOCK].astype(o_ref.dtype)


def _attn_kernel(sink_ref, q_ref, kc_ref, vc_ref, kp_ref, ko_ref, kn_ref, vp_ref, vo_ref, vn_ref, bias_ref, o_ref, *,
                 slot, n_ctx_blocks):
    hkv = pl.program_id(1)
    step = pl.program_id(2)
    rows_q = ATTN_GROUP * ATTN_BLOCK
    grp = lax.broadcasted_iota(jnp.int32, (rows_q, 1), 0) // ATTN_BLOCK
    sink = jnp.zeros((rows_q, 1), F32)
    for g in range(ATTN_GROUP):
        sink = jnp.where(grp == g, sink_ref[slot, hkv * ATTN_GROUP + g] * LOG2_E, sink)

    @pl.when(step < n_ctx_blocks)
    def _():
        _attn_block(q_ref, o_ref, sink, (kc_ref[...],), (vc_ref[...],), None)

    @pl.when(step >= n_ctx_blocks)
    def _():
        kb = jnp.concatenate([kp_ref[...], ko_ref[...], kn_ref[...]], axis=0)
        vb = jnp.concatenate([vp_ref[...], vo_ref[...], vn_ref[...]], axis=0)
        _attn_block(q_ref, o_ref, sink, (kc_ref[...], kb), (vc_ref[...], vb), bias_ref[...])


def _band_bias():
    i = np.arange(ATTN_BLOCK)[:, None]
    j = np.arange(3 * ATTN_BLOCK)[None, :]
    window = (j >= i) & (j <= i + 2 * ATTN_BLOCK)
    first = window & (j >= ATTN_BLOCK)
    last = window & (j < 2 * ATTN_BLOCK)
    return jnp.asarray(np.where(np.stack([first, window, last]), 0.0, NEG_INF), dtype=F32)


def _attention(p, sink, slot, rows):
    r = p.shape[0]
    dq = ATTN_HEADS * HEAD_DIM
    gw = ATTN_GROUP * HEAD_DIM
    kcol = dq // HEAD_DIM
    vcol = kcol + ATTN_KV_HEADS
    L, S, B = rows.ctx_len, rows.seq, rows.batch
    nb = S // ATTN_BLOCK
    nbc = L // ATTN_BLOCK
    lat0 = rows.n_ctx // ATTN_BLOCK
    assert nb >= 2

    def q_block(b, s):
        return jnp.where(s < nbc, b * nbc + s, lat0 + b * nb + (s - nbc))

    def band(col0, shift):
        return pl.BlockSpec((ATTN_BLOCK, HEAD_DIM),
                            lambda b, h, s: (lat0 + b * nb + jnp.clip(s - nbc + shift, 0, nb - 1), col0 + h))

    return pl.pallas_call(
        functools.partial(_attn_kernel, slot=slot, n_ctx_blocks=nbc),
        grid=(B, ATTN_KV_HEADS, nbc + nb),
        in_specs=[pl.BlockSpec(memory_space=pltpu.SMEM),
                  pl.BlockSpec((ATTN_BLOCK, gw), lambda b, h, s: (q_block(b, s), h)),
                  pl.BlockSpec((L, HEAD_DIM), lambda b, h, s: (b, kcol + h)),
                  pl.BlockSpec((L, HEAD_DIM), lambda b, h, s: (b, vcol + h)),
                  band(kcol, -1), band(kcol, 0), band(kcol, 1), band(vcol, -1), band(vcol, 0), band(vcol, 1),
                  pl.BlockSpec((None, ATTN_BLOCK, 3 * ATTN_BLOCK),
                               lambda b, h, s: (jnp.where(s <= nbc, 0, jnp.where(s == nbc + nb - 1, 2, 1)), 0, 0))],
        out_specs=pl.BlockSpec((ATTN_BLOCK, gw), lambda b, h, s: (q_block(b, s), h)),
        out_shape=jax.ShapeDtypeStruct((r, dq), BF16),
        compiler_params=_cparams(3),
        name="attention",
    )(sink, p, p, p, p, p, p, p, p, p, _band_bias())


def _dft_cs(n, scale):
    k = np.arange(n)
    ang = 2.0 * np.pi * ((k[:, None] * k[None, :]) % n) / n
    return np.cos(ang) * scale, np.sin(ang) * scale


def _chan_dft_kernel(u_ref, m_ref, o_ref, *, gd):
    for g in range(u_ref.shape[1] // gd):
        v = _dot(u_ref[:, g * gd:(g + 1) * gd], m_ref[...])
        o_ref[0, :, g * gd:(g + 1) * gd] = v[:, :gd].astype(o_ref.dtype)
        o_ref[1, :, g * gd:(g + 1) * gd] = v[:, gd:].astype(o_ref.dtype)


def _chan_dft(u, row0, n_rows, mat):
    d = u.shape[1]
    gd = d // FNET_GROUPS
    t0 = row0 // ROW_TILE
    return pl.pallas_call(
        functools.partial(_chan_dft_kernel, gd=gd),
        grid=(n_rows // ROW_TILE,),
        in_specs=[pl.BlockSpec((ROW_TILE, d), lambda i: (t0 + i, 0)),
                  pl.BlockSpec((gd, 2 * gd), lambda i: (0, 0))],
        out_specs=pl.BlockSpec((2, ROW_TILE, d), lambda i: (0, i, 0)),
        out_shape=jax.ShapeDtypeStruct((2, n_rows, d), BF16),
        compiler_params=_cparams(1),
        name="fnet_chan_dft",
    )(u, mat)


def _seq_dft_a_kernel(x_ref, chan_ref, m_ref, tc_ref, ts_ref, o_ref, *, gd):
    n1 = x_ref.shape[0]
    parts = [_dot(x_ref[:, g * gd:(g + 1) * gd], chan_ref[...]) for g in range(x_ref.shape[1] // gd)]
    vr = jnp.concatenate([p[:, :gd] for p in parts], axis=1).astype(BF16)
    vi = jnp.concatenate([p[:, gd:] for p in parts], axis=1).astype(BF16)
    z = _dot(m_ref[...], jnp.concatenate([vr, vi], axis=0))
    zr, zi = z[:n1], z[n1:]
    tc, ts = tc_ref[...], ts_ref[...]
    for c in range(z.shape[1] // LANES):
        sl = slice(c * LANES, (c + 1) * LANES)
        o_ref[0, :, sl] = (zr[:, sl] * tc + zi[:, sl] * ts).astype(o_ref.dtype)
        o_ref[1, :, sl] = (zi[:, sl] * tc - zr[:, sl] * ts).astype(o_ref.dtype)


def _seq_dft_c_kernel(x_ref, m_ref, o_ref):
    x = jnp.concatenate([x_ref[0], x_ref[1]], axis=0)
    o_ref[...] = _dot(m_ref[...], x).astype(o_ref.dtype)


def _seq_dft_c_blocked_kernel(x_ref, m_ref, o_ref):
    two, n2, kb, d = x_ref.shape
    x = x_ref[...].reshape(two * n2 * kb, d)
    o_ref[...] = _dot(m_ref[...], x).reshape(n2, kb, d).astype(o_ref.dtype)


def _fourier_tokens(u, rows):
    d = u.shape[1]
    gd = d // FNET_GROUPS
    B, S, L = rows.batch, rows.seq, rows.ctx_len
    n2 = GRID_W
    n1 = S // n2
    cc, sc = _dft_cs(gd, gd ** -0.5)
    chan = jnp.asarray(np.concatenate([cc, -sc], axis=1), dtype=BF16)

    x_lat = u[rows.n_ctx:].reshape(B, n1, n2 * d)
    c1, s1 = _dft_cs(n1, n1 ** -0.5)
    m1 = jnp.asarray(np.block([[c1, s1], [-s1, c1]]), dtype=BF16)
    ang = 2.0 * np.pi * (np.arange(n2)[:, None] * np.arange(n1)[None, :]) / S
    tw_c = jnp.asarray(np.broadcast_to(np.cos(ang)[:, :, None], (n2, n1, LANES)), dtype=F32)
    tw_s = jnp.asarray(np.broadcast_to(np.sin(ang)[:, :, None], (n2, n1, LANES)), dtype=F32)
    z = pl.pallas_call(
        functools.partial(_seq_dft_a_kernel, gd=gd),
        grid=(B, n2),
        in_specs=[pl.BlockSpec((None, n1, d), lambda b, t: (b, 0, t)),
                  pl.BlockSpec((gd, 2 * gd), lambda b, t: (0, 0)),
                  pl.BlockSpec((2 * n1, 2 * n1), lambda b, t: (0, 0)),
                  pl.BlockSpec((None, n1, LANES), lambda b, t: (t, 0, 0)),
                  pl.BlockSpec((None, n1, LANES), lambda b, t: (t, 0, 0))],
        out_specs=pl.BlockSpec((2, None, None, n1, d), lambda b, t: (0, b, t, 0, 0)),
        out_shape=jax.ShapeDtypeStruct((2, B, n2, n1, d), BF16),
        compiler_params=_cparams(2),
        name="fnet_seq_dft_a",
    )(x_lat, chan, m1, tw_c, tw_s)
    c2, s2 = _dft_cs(n2, n2 ** -0.5)
    kb = DFT_POS_BLOCK
    m2 = jnp.asarray(np.kron(np.concatenate([c2, s2], axis=1), np.eye(kb)), dtype=BF16)
    y_lat = pl.pallas_call(
        _seq_dft_c_blocked_kernel,
        grid=(B, n1 // kb),
        in_specs=[pl.BlockSpec((2, None, n2, kb, d), lambda b, j: (0, b, 0, j, 0)),
                  pl.BlockSpec((n2 * kb, 2 * n2 * kb), lambda b, j: (0, 0))],
        out_specs=pl.BlockSpec((None, n2, kb, d), lambda b, j: (b, 0, j, 0)),
        out_shape=jax.ShapeDtypeStruct((B, n2, n1, d), BF16),
        compiler_params=_cparams(2),
        name="fnet_seq_dft_c",
    )(z, m2)

    vc = _chan_dft(u, 0, B * L, chan).reshape(2, B, L, d)
    cl, sl = _dft_cs(L, L ** -0.5)
    ml = jnp.asarray(np.concatenate([cl, sl], axis=1), dtype=BF16)
    y_ctx = pl.pallas_call(
        _seq_dft_c_kernel,
        grid=(B, 1),
        in_specs=[pl.BlockSpec((2, None, L, d), lambda b, j: (0, b, 0, 0)),
                  pl.BlockSpec((L, 2 * L), lambda b, j: (0, 0))],
        out_specs=pl.BlockSpec((None, L, d), lambda b, j: (b, 0, 0)),
        out_shape=jax.ShapeDtypeStruct((B, L, d), BF16),
        compiler_params=_cparams(2),
        name="fnet_ctx_dft",
    )(vc, ml)
    return jnp.concatenate([y_ctx.reshape(B * L, d), y_lat.reshape(B * S, d)], axis=0)


def _hgrn_scan_kernel(q_ref, f_ref, v_ref, lb_ref, tri_ref, o_ref, st_ref, *, reverse):
    @pl.when(pl.program_id(1) == 0)
    def _():
        st_ref[...] = jnp.zeros(st_ref.shape, st_ref.dtype)

    c = HGRN_CHUNK
    n_sub = q_ref.shape[0] // c
    lb = lb_ref[...]
    tri = tri_ref[...]
    ti = lax.broadcasted_iota(jnp.int32, (c, c), 0)
    si = lax.broadcasted_iota(jnp.int32, (c, c), 1)
    keep = (si >= ti) if reverse else (si <= ti)
    order = range(n_sub - 1, -1, -1) if reverse else range(n_sub)
    for sub in order:
        rs = slice(sub * c, (sub + 1) * c)
        q = q_ref[rs, :].astype(F32)
        f = lb + (1.0 - lb) * _sigmoid(f_ref[rs, :].astype(F32))
        k = 1.0 - f
        bsum = jnp.dot(tri, jnp.log(f), preferred_element_type=F32, precision=lax.Precision.HIGHEST)
        b_end = bsum[0:1, :] if reverse else bsum[c - 1:c, :]
        decay = jnp.exp(b_end)
        q_in = (q * jnp.exp(bsum)).astype(BF16)
        k_inf = k * jnp.exp(-bsum)
        k_in = k_inf.astype(BF16)
        k_out = (k_inf * decay).astype(BF16)
        v = v_ref[rs, :]
        vt = v.astype(F32)
        for h in range(HGRN_HEADS):
            ks = slice(h * HGRN_DK, (h + 1) * HGRN_DK)
            vs = slice(h * HGRN_DV, (h + 1) * HGRN_DV)
            a = jnp.where(keep, _dot_nt(q_in[:, ks], k_in[:, ks]), 0.0)
            st = st_ref[h]
            o = _dot(a.astype(BF16), v[:, vs]) + _dot_nt(q_in[:, ks], st.astype(BF16))
            o_ref[rs, vs] = o
            st_ref[h] = st * decay[:, ks] + _dot(vt[:, vs].T.astype(BF16), k_out[:, ks])


def _hgrn_scan(pm, lb, rows, reverse):
    r = pm.shape[0]
    hk = HGRN_HEADS * HGRN_DK
    B, S, L = rows.batch, rows.seq, rows.ctx_len
    cs, ls = L // SCAN_ROWS, S // SCAN_ROWS
    direction = 1 if reverse else 0

    def row_block(b, s):
        if reverse:
            ctx = b * cs + (cs - 1 - s)
            lat = B * cs + b * ls + (ls - 1 - (s - cs))
        else:
            ctx = b * cs + s
            lat = B * cs + b * ls + (s - cs)
        return jnp.where(s < cs, ctx, lat)

    c = HGRN_CHUNK
    tri_np = np.triu(np.ones((c, c))) if reverse else np.tril(np.ones((c, c)))
    tri = jnp.asarray(tri_np, dtype=F32)
    col = lambda j: pl.BlockSpec((SCAN_ROWS, hk), lambda b, s: (row_block(b, s), j))
    return pl.pallas_call(
        functools.partial(_hgrn_scan_kernel, reverse=reverse),
        grid=(B, cs + ls),
        in_specs=[col(0), col(1 + direction), col(3),
                  pl.BlockSpec((None, 1, hk), lambda b, s: (direction, 0, 0)),
                  pl.BlockSpec((c, c), lambda b, s: (0, 0))],
        out_specs=pl.BlockSpec((SCAN_ROWS, HGRN_HEADS * HGRN_DV), lambda b, s: (row_block(b, s), 0)),
        out_shape=jax.ShapeDtypeStruct((r, HGRN_HEADS * HGRN_DV), F32),
        scratch_shapes=[pltpu.VMEM((HGRN_HEADS, HGRN_DV, HGRN_DK), F32)],
        compiler_params=_cparams(2),
        name="hgrn_scan_bwd" if reverse else "hgrn_scan_fwd",
    )(pm, pm, pm, lb, tri)


def _hgrn_readout_kernel(of_ref, ob_ref, g_ref, ng_ref, o_ref):
    for h in range(HGRN_HEADS):
        vs = slice(h * HGRN_DV, (h + 1) * HGRN_DV)
        o = of_ref[:, vs] + ob_ref[:, vs]
        o = o * lax.rsqrt(jnp.mean(o * o, axis=-1, keepdims=True) + RMS_EPS) * ng_ref[:, vs]
        o_ref[:, vs] = (o * g_ref[:, vs].astype(F32)).astype(o_ref.dtype)


def _hgrn_readout(o_f, o_b, pm, norm_g, slot):
    r, d = o_f.shape
    gcol = pm.shape[1] // d - 1
    return pl.pallas_call(
        _hgrn_readout_kernel,
        grid=(r // ROW_TILE,),
        in_specs=[pl.BlockSpec((ROW_TILE, d), lambda i: (i, 0)),
                  pl.BlockSpec((ROW_TILE, d), lambda i: (i, 0)),
                  pl.BlockSpec((ROW_TILE, d), lambda i: (i, gcol)),
                  pl.BlockSpec((None, 1, d), lambda i: (slot, 0, 0))],
        out_specs=pl.BlockSpec((ROW_TILE, d), lambda i: (i, 0)),
        out_shape=jax.ShapeDtypeStruct((r, d), BF16),
        compiler_params=_cparams(1),
        name="hgrn_readout",
    )(o_f, o_b, pm, norm_g)


def _moe_plan(route, n_tiles):
    e_flat = jnp.concatenate([route[:, 0], route[:, 1]]).astype(jnp.int32)
    onehot = (e_flat[:, None] == jnp.arange(N_EXPERTS, dtype=jnp.int32)[None, :]).astype(jnp.int32)
    csum = jnp.cumsum(onehot, axis=0)
    counts = csum[-1]
    rank = jnp.sum(csum * onehot, axis=1) - 1
    padded = ((counts + MOE_ROW_TILE - 1) // MOE_ROW_TILE) * MOE_ROW_TILE
    ends = jnp.cumsum(padded)
    starts = ends - padded
    dest = jnp.sum(starts[None, :] * onehot, axis=1) + rank
    n_used = (ends[-1] // MOE_ROW_TILE).astype(jnp.int32)
    tile = jnp.arange(n_tiles, dtype=jnp.int32)
    te = jnp.sum((ends[None, :] <= (tile * MOE_ROW_TILE)[:, None]).astype(jnp.int32), axis=1)
    te = jnp.minimum(te, N_EXPERTS - 1)
    te_last = jnp.sum(jnp.where(tile == n_used - 1, te, 0))
    te = jnp.where(tile < n_used, te, te_last).astype(jnp.int32)
    pad = jnp.stack([starts + counts, ends]).astype(jnp.int32)
    return dest, te, n_used.reshape(1), pad.reshape(-1)


def _moe_gather_kernel(dest_ref, nu_ref, pad_ref, src_ref, o_ref, tok_ref, buf_ref, sem, *, n_tokens, row0):
    i = pl.program_id(0)
    n_used = nu_ref[0]

    def row_copy(slot, r, t):
        return pltpu.make_async_copy(src_ref.at[pl.ds(t, 1), :], buf_ref.at[slot, pl.ds(r, 1), :], sem.at[slot])

    def issue_tile(tile):
        slot = tile % 2
        base = tile * MOE_ROW_TILE

        def issue(r, carry):
            row_copy(slot, r, tok_ref[base + r]).start()
            return carry

        lax.fori_loop(0, MOE_ROW_TILE, issue, 0, unroll=8)

    @pl.when(i == 0)
    def _():
        def fill(p, carry):
            tok_ref[p] = row0
            return carry

        def place(t, carry):
            tok_ref[dest_ref[t]] = row0 + t
            tok_ref[dest_ref[n_tokens + t]] = row0 + t
            return carry

        for e in range(N_EXPERTS):
            lax.fori_loop(pad_ref[e], pad_ref[N_EXPERTS + e], fill, 0)
        lax.fori_loop(0, n_tokens, place, 0, unroll=8)
        issue_tile(i)

    @pl.when(i + 1 < n_used)
    def _():
        issue_tile(i + 1)

    @pl.when(i < n_used)
    def _():
        slot = i % 2

        def drain(r, carry):
            row_copy(slot, r, 0).wait()
            return carry

        lax.fori_loop(0, MOE_ROW_TILE, drain, 0, unroll=8)
        o_ref[...] = buf_ref[slot].astype(o_ref.dtype)

    @pl.when(i >= n_used)
    def _():
        o_ref[...] = jnp.zeros(o_ref.shape, o_ref.dtype)


def _moe_gather(dest, n_used, pad, u, n_tiles, row0):
    d = u.shape[1]
    p_rows = n_tiles * MOE_ROW_TILE
    return pl.pallas_call(
        functools.partial(_moe_gather_kernel, n_tokens=dest.shape[0] // 2, row0=row0),
        grid_spec=pltpu.PrefetchScalarGridSpec(
            num_scalar_prefetch=3,
            grid=(n_tiles,),
            in_specs=[pl.BlockSpec(memory_space=pl.ANY)],
            out_specs=pl.BlockSpec((MOE_ROW_TILE, d), lambda i, dst, nu, pd: (i, 0)),
            scratch_shapes=[pltpu.SMEM((p_rows,), jnp.int32),
                            pltpu.VMEM((2, MOE_ROW_TILE, d), F32), pltpu.SemaphoreType.DMA((2,))]),
        out_shape=jax.ShapeDtypeStruct((p_rows, d), BF16),
        compiler_params=_cparams(1),
        name="moe_gather",
    )(dest, n_used, pad, u)


def _expert_changed(te_ref):
    i = pl.program_id(1)
    return (i == 0) | (te_ref[i] != te_ref[jnp.maximum(i - 1, 0)])


def _gmm_swiglu_kernel(te_ref, nu_ref, x_ref, wa_ref, wb_ref, w2_ref, o_ref, w2bf_ref, wbfa_ref, wbfb_ref):
    i = pl.program_id(1)

    @pl.when(_expert_changed(te_ref))
    def _():
        wbfa_ref[...] = wa_ref[...].astype(BF16)
        wbfb_ref[...] = wb_ref[...].astype(BF16)

    @pl.when(i < nu_ref[0])
    def _():
        x = x_ref[...]
        o_ref[...] = (_silu(_dot(x, wbfa_ref[...])) * _dot(x, wbfb_ref[...])).astype(o_ref.dtype)

    @pl.when(i >= nu_ref[0])
    def _():
        o_ref[...] = jnp.zeros(o_ref.shape, o_ref.dtype)

    w2bf_ref[...] = w2_ref[...].astype(BF16)


def _gmm_down_kernel(te_ref, nu_ref, x_ref, w_ref, o_ref):
    i = pl.program_id(1)

    @pl.when(i < nu_ref[0])
    def _():
        o_ref[...] = _dot(x_ref[...], w_ref[...])

    @pl.when(i >= nu_ref[0])
    def _():
        o_ref[...] = jnp.zeros(o_ref.shape, o_ref.dtype)


def _moe_experts(xg, te, n_used, w13, w2, layer, n_tiles):
    p_rows, d = xg.shape
    n_layers, n_exp, f, d_out = w2.shape
    tn = COL_TILE
    nf = f // tn
    used = lambda i, nu: jnp.minimum(i, nu[0] - 1)
    ch = _side_cast_rows(n_exp * f, nf * n_tiles)
    n_chunks = n_exp * f // ch
    chunk = lambda n, i: jnp.minimum(n * n_tiles + i, n_chunks - 1)
    hidden, w2_bf16 = pl.pallas_call(
        _gmm_swiglu_kernel,
        grid_spec=pltpu.PrefetchScalarGridSpec(
            num_scalar_prefetch=2,
            grid=(nf, n_tiles),
            in_specs=[pl.BlockSpec((MOE_ROW_TILE, d), lambda n, i, te, nu: (used(i, nu), 0)),
                      pl.BlockSpec((None, None, d, tn), lambda n, i, te, nu: (layer, te[i], 0, n)),
                      pl.BlockSpec((None, None, d, tn), lambda n, i, te, nu: (layer, te[i], 0, n + nf)),
                      pl.BlockSpec((None, ch, d_out), lambda n, i, te, nu: (layer, chunk(n, i), 0))],
            out_specs=[pl.BlockSpec((MOE_ROW_TILE, tn), lambda n, i, te, nu: (i, n)),
                       pl.BlockSpec((ch, d_out), lambda n, i, te, nu: (chunk(n, i), 0))],
            scratch_shapes=[pltpu.VMEM((d, tn), BF16), pltpu.VMEM((d, tn), BF16)]),
        out_shape=[jax.ShapeDtypeStruct((p_rows, f), BF16), jax.ShapeDtypeStruct((n_exp * f, d_out), BF16)],
        compiler_params=_cparams(2),
        name="moe_w13",
    )(te, n_used, xg, w13, w13, w2.reshape(n_layers, n_exp * f, d_out))
    tn2 = MOE_W2_COL_TILE
    return pl.pallas_call(
        _gmm_down_kernel,
        grid_spec=pltpu.PrefetchScalarGridSpec(
            num_scalar_prefetch=2,
            grid=(d_out // tn2, n_tiles),
            in_specs=[pl.BlockSpec((MOE_ROW_TILE, f), lambda n, i, te, nu: (used(i, nu), 0)),
                      pl.BlockSpec((None, f, tn2), lambda n, i, te, nu: (te[i], 0, n))],
            out_specs=pl.BlockSpec((MOE_ROW_TILE, tn2), lambda n, i, te, nu: (i, n)),
            scratch_shapes=[]),
        out_shape=jax.ShapeDtypeStruct((p_rows, d_out), F32),
        compiler_params=_cparams(2),
        name="moe_w2",
    )(te, n_used, hidden, w2_bf16.reshape(n_exp, f, d_out))


def _moe_combine_kernel(p1_ref, p2_ref, y_ref, h_ref, g_ref, route_ref, *rest, mode):
    if mode == "final":
        ng_ref, o_ref, buf_ref, sem = rest
    else:
        ng_ref, sh_ref, sc_ref, o_ref, u_ref, buf_ref, sem = rest
    i = pl.program_id(0)

    def row_copy(slot, k, r, p):
        return pltpu.make_async_copy(y_ref.at[pl.ds(p, 1), :], buf_ref.at[slot, k, pl.ds(r, 1), :], sem.at[slot])

    def issue_tile(tile):
        slot = tile % 2
        base = tile * COMBINE_ROW_TILE

        def issue(r, carry):
            row_copy(slot, 0, r, p1_ref[base + r]).start()
            row_copy(slot, 1, r, p2_ref[base + r]).start()
            return carry

        lax.fori_loop(0, COMBINE_ROW_TILE, issue, 0, unroll=8)

    @pl.when(i == 0)
    def _():
        issue_tile(i)

    @pl.when(i + 1 < pl.num_programs(0))
    def _():
        issue_tile(i + 1)

    slot = i % 2

    def drain(r, carry):
        row_copy(slot, 0, r, 0).wait()
        row_copy(slot, 1, r, 0).wait()
        return carry

    lax.fori_loop(0, COMBINE_ROW_TILE, drain, 0, unroll=8)
    w1 = route_ref[:, 2:3]
    w2 = route_ref[:, 3:4]
    h_new = h_ref[...] + g_ref[...] * (w1 * buf_ref[slot, 0] + w2 * buf_ref[slot, 1])
    if mode == "final":
        o_ref[...] = _rms_norm(h_new, ng_ref[...])
    else:
        o_ref[...] = h_new
        u_ref[...] = (_rms_norm(h_new, ng_ref[...]) * (1.0 + sc_ref[...]) + sh_ref[...]).astype(u_ref.dtype)


def _moe_combine(pos1, pos2, y, h, mods, route, rows, mod_layer, row0, post):
    d = h.shape[1]
    r = h.shape[0] - row0
    g = rows.groups
    tr = COMBINE_ROW_TILE
    t0 = row0 // tr
    mod = lambda layer, which: pl.BlockSpec(
        (None, 1, d), lambda i, p1, p2: ((layer * 6 + which) * g + rows.group(t0 + i, tr), 0, 0))
    tile = pl.BlockSpec((tr, d), lambda i, p1, p2: (i, 0))
    in_specs = [pl.BlockSpec(memory_space=pl.ANY),
                pl.BlockSpec((tr, d), lambda i, p1, p2: (t0 + i, 0)),
                mod(mod_layer, 5),
                pl.BlockSpec((tr, LANES), lambda i, p1, p2: (t0 + i, 0))]
    if post[0] == "final":
        in_specs.append(pl.BlockSpec((1, d), lambda i, p1, p2: (0, 0)))
        extra = (post[1],)
        out_specs, out_shape = tile, jax.ShapeDtypeStruct((r, d), F32)
    else:
        _, norm_w, nxt = post
        in_specs += [pl.BlockSpec((None, 1, d), lambda i, p1, p2: (nxt, 0, 0)), mod(nxt, 0), mod(nxt, 1)]
        extra = (norm_w, mods, mods)
        out_specs = [tile, tile]
        out_shape = [jax.ShapeDtypeStruct((r, d), F32), jax.ShapeDtypeStruct((r, d), BF16)]
    return pl.pallas_call(
        functools.partial(_moe_combine_kernel, mode=post[0]),
        grid_spec=pltpu.PrefetchScalarGridSpec(
            num_scalar_prefetch=2,
            grid=(r // tr,),
            in_specs=in_specs,
            out_specs=out_specs,
            scratch_shapes=[pltpu.VMEM((2, 2, tr, d), F32), pltpu.SemaphoreType.DMA((2,))]),
        out_shape=out_shape,
        compiler_params=_cparams(1),
        name="moe_combine",
    )(pos1, pos2, y, h, mods, route, *extra)


def _moe_ffn(h, u, route, mods, rows, layer, w13, w2, moe_layer, row0, post):
    r = h.shape[0] - row0
    n_tiles = -(-(2 * r + N_EXPERTS * (MOE_ROW_TILE - 1)) // MOE_ROW_TILE)
    dest, te, n_used, pad = _moe_plan(route[row0:], n_tiles)
    xg = _moe_gather(dest, n_used, pad, u, n_tiles, row0)
    y = _moe_experts(xg, te, n_used, w13, w2, moe_layer, n_tiles)
    return _moe_combine(dest[:r], dest[r:], y, h, mods, route, rows, layer, row0, post)


def _hgrn_lower_bounds(lb_logits, layer):
    gamma = jax.nn.softmax(lb_logits.astype(F32), axis=0)
    lb = jnp.cumsum(gamma, axis=0) - gamma[0]
    return lb[layer]


def kernel(x, c, ctx, c_ctx, ada_w, ada_b, norm_mix, norm_ffn, norm_final, attn_wqkv, attn_wo, attn_sink, fnet_wo,
           hgrn_win, hgrn_lb, hgrn_norm, hgrn_wo, ffn_w13, ffn_w2, moe_router, moe_w13, moe_w2):
    B, S, D = x.shape
    L = ctx.shape[1]
    depth = ada_w.shape[0]
    rows = _Rows(B, S, L)
    G = rows.groups

    cond = jnp.concatenate([c_ctx[None, :], c], axis=0)
    cond = jnp.pad(_silu(cond), ((0, 16 - G), (0, 0))).astype(BF16)
    mods = _ada_mods(cond, ada_w, ada_b)
    mods = mods[:, :G, :].reshape(depth, G, 6, D).transpose(0, 2, 1, 3).reshape(depth * 6 * G, 1, D)

    norm_mix3 = norm_mix.reshape(depth, 1, D)
    h, u = _input_norm(ctx.reshape(B * L, D), x.reshape(B * S, D), norm_mix3, mods, rows)
    norm_ffn3 = norm_ffn.reshape(depth, 1, D)
    rope_tables = _rope_tables(rows)
    router_pad = jnp.pad(moe_router, ((0, 0), (0, 0), (0, LANES - moe_router.shape[-1]))).astype(BF16)

    for i in range(depth):
        kind, slot = i % N_MIXERS, i // N_MIXERS
        j = i // 2
        dense = i % 2 == 0
        if u is None:
            u = _norm_mod(h, norm_mix3, mods, rows, i, 0)
        if kind == 0:
            p = _mm_qkv_rope(u, attn_wqkv, slot, rope_tables, rows)
            o, w_out, name = _attention(p, attn_sink, slot, rows), attn_wo, "attn_out"
        elif kind == 1:
            o, w_out, name = _fourier_tokens(u, rows), fnet_wo, "fnet_out"
        else:
            lb = _hgrn_lower_bounds(hgrn_lb, i).reshape(2, 1, HGRN_HEADS * HGRN_DK)
            pm = _mm_hgrn_in(u, hgrn_win, slot)
            o_f = _hgrn_scan(pm, lb, rows, reverse=False)
            o_b = _hgrn_scan(pm, lb, rows, reverse=True)
            o, w_out, name = _hgrn_readout(o_f, o_b, pm, hgrn_norm.reshape(-1, 1, D), slot), hgrn_wo, "hgrn_out"
        u = None

        if dense:
            h, v = _mixer_out(o, w_out, slot, h, mods, rows, i, norm_ffn3, name=name)
            g, w2_bf16 = _mm_swiglu(v, ffn_w13, ffn_w2, j)
            h = _mm_resid_bf16w(g, w2_bf16, h, mods, rows, i, 5, FFN_W2_COL_TILE, name="ffn_out")
            continue
        h, v, route = _mixer_out(o, w_out, slot, h, mods, rows, i, norm_ffn3, router_pad, j, name=name)
        if i == depth - 1:
            out = _moe_ffn(h, v, route, mods, rows, i, moe_w13, moe_w2, j, rows.n_ctx,
                           ("final", norm_final.reshape(1, D)))
            return out.reshape(B, S, D)
        h, u = _moe_ffn(h, v, route, mods, rows, i, moe_w13, moe_w2, j, 0, ("next", norm_mix3, i + 1))

    return _final_norm(h, norm_final, rows).reshape(B, S, D)
```

```python
import functools

import numpy as np
import jax
import jax.numpy as jnp
from jax import lax
from jax.experimental import pallas as pl
from jax.experimental.pallas import tpu as pltpu

F32 = jnp.float32
BF16 = jnp.bfloat16

N_MIXERS = 3
RMS_EPS = 1e-6
NEG_INF = -1e30
LOG2_E = 1.4426950408889634
GRID_W = 64
ATTN_HEADS = 16
ATTN_KV_HEADS = 4
ATTN_GROUP = ATTN_HEADS // ATTN_KV_HEADS
HEAD_DIM = 128
ATTN_BLOCK = 128
ROPE_THETA = 10000.0
FNET_GROUPS = 8
HGRN_HEADS = 16
HGRN_DK = 128
HGRN_DV = 128
HGRN_CHUNK = 64
N_EXPERTS = 8

LANES = 128
ROW_TILE = 512
COL_TILE = 1024
QKV_COL_TILE = 1536
SWIGLU_COL_TILE = 512
FFN_W2_COL_TILE = 1024
MOE_W2_COL_TILE = 1024
MOE_ROW_TILE = 512
COMBINE_ROW_TILE = 256
SCAN_ROWS = 256
DFT_T2_BLOCK = 2
DFT_POS_BLOCK = 16
VMEM_LIMIT_BYTES = 56 * 1024 * 1024


def _cparams(n_axes):
    return pltpu.CompilerParams(dimension_semantics=("arbitrary",) * n_axes,
                                vmem_limit_bytes=VMEM_LIMIT_BYTES)


def _dot(a, b):
    return jnp.dot(a, b, preferred_element_type=F32)


def _dot_nt(a, b):
    return lax.dot_general(a, b, (((1,), (1,)), ((), ())), preferred_element_type=F32)


def _sigmoid(x):
    return 1.0 / (1.0 + jnp.exp(-x))


def _silu(x):
    return x * _sigmoid(x)


def _ada_kernel(a_ref, w_ref, b_ref, o_ref):
    o_ref[...] = _dot(a_ref[...], w_ref[...].astype(BF16)) + b_ref[...]


def _ada_mods(cond_rows, ada_w, ada_b):
    depth, d, n6 = ada_w.shape
    rows = cond_rows.shape[0]
    tn = 1024
    return pl.pallas_call(
        _ada_kernel,
        grid=(depth, n6 // tn),
        in_specs=[pl.BlockSpec((rows, d), lambda l, n: (0, 0)),
                  pl.BlockSpec((None, d, tn), lambda l, n: (l, 0, n)),
                  pl.BlockSpec((None, 1, tn), lambda l, n: (l, 0, n))],
        out_specs=pl.BlockSpec((None, rows, tn), lambda l, n: (l, 0, n)),
        out_shape=jax.ShapeDtypeStruct((depth, rows, n6), F32),
        compiler_params=_cparams(2),
        name="ada_mods",
    )(cond_rows, ada_w, ada_b.reshape(depth, 1, n6))


def _rms_norm(h, gain):
    return h * lax.rsqrt(jnp.mean(h * h, axis=-1, keepdims=True) + RMS_EPS) * gain


def _norm_mod_kernel(h_ref, g_ref, sh_ref, sc_ref, o_ref):
    o_ref[...] = (_rms_norm(h_ref[...], g_ref[...]) * (1.0 + sc_ref[...]) + sh_ref[...]).astype(o_ref.dtype)


def _input_norm_kernel(ctx_ref, x_ref, g_ref, sh_ref, sc_ref, h_ref, u_ref, *, ctx_tiles):
    def emit(src_ref):
        h = src_ref[...]
        h_ref[...] = h
        u_ref[...] = (_rms_norm(h, g_ref[...]) * (1.0 + sc_ref[...]) + sh_ref[...]).astype(u_ref.dtype)

    @pl.when(pl.program_id(0) < ctx_tiles)
    def _():
        emit(ctx_ref)

    @pl.when(pl.program_id(0) >= ctx_tiles)
    def _():
        emit(x_ref)


def _top2_route(u, router):
    logits = _dot(u.astype(BF16), router)
    lane = lax.broadcasted_iota(jnp.int32, logits.shape, 1)
    l1 = jnp.where(lane < N_EXPERTS, logits, NEG_INF)
    m1 = jnp.max(l1, axis=-1, keepdims=True)
    i1 = jnp.min(jnp.where(l1 == m1, lane, LANES), axis=-1, keepdims=True)
    l2 = jnp.where(lane == i1, NEG_INF, l1)
    m2 = jnp.max(l2, axis=-1, keepdims=True)
    i2 = jnp.min(jnp.where(l2 == m2, lane, LANES), axis=-1, keepdims=True)
    e2 = jnp.exp(m2 - m1)
    w1 = 1.0 / (1.0 + e2)
    w2 = e2 * w1
    return jnp.where(lane == 0, i1.astype(F32),
                     jnp.where(lane == 1, i2.astype(F32), jnp.where(lane == 2, w1, jnp.where(lane == 3, w2, 0.0))))


def _final_norm_kernel(h_ref, g_ref, o_ref):
    o_ref[...] = _rms_norm(h_ref[...], g_ref[...])


def _cast_weight(w_ref, wbf_ref):
    @pl.when(pl.program_id(1) == 0)
    def _():
        wbf_ref[...] = w_ref[...].astype(BF16)


def _mm_rope_kernel(a_ref, w_ref, cos_ref, sa_ref, sb_ref, o_ref, wbf_ref, *, n_col_tiles, n_q_heads, n_rope_heads,
                    q_scale):
    _cast_weight(w_ref, wbf_ref)
    acc = _dot(a_ref[...], wbf_ref[...])
    heads_per_tile = acc.shape[1] // HEAD_DIM
    for tile in range(n_col_tiles):
        @pl.when(pl.program_id(0) == tile)
        def _(tile=tile):
            for c in range(heads_per_tile):
                head = tile * heads_per_tile + c
                t = acc[:, c * HEAD_DIM:(c + 1) * HEAD_DIM]
                if head < n_rope_heads:
                    t = t * cos_ref[...] + pltpu.roll(t, HEAD_DIM - 32, 1) * sa_ref[...] + pltpu.roll(t, 32, 1) * sb_ref[...]
                if head < n_q_heads:
                    t = t * q_scale
                o_ref[:, c * HEAD_DIM:(c + 1) * HEAD_DIM] = t.astype(o_ref.dtype)


def _mm_swiglu_kernel(a_ref, wa_ref, wb_ref, w2_ref, o_ref, w2bf_ref, wbfa_ref, wbfb_ref):
    _cast_weight(wa_ref, wbfa_ref)
    _cast_weight(wb_ref, wbfb_ref)
    a = a_ref[...]
    ga = _dot(a, wbfa_ref[...])
    gb = _dot(a, wbfb_ref[...])
    o_ref[...] = (_silu(ga) * gb).astype(o_ref.dtype)
    w2bf_ref[...] = w2_ref[...].astype(BF16)


def _mm_hgrn_in_kernel(a_ref, w_ref, o_ref, wbf_ref, *, tiles_per_segment):
    _cast_weight(w_ref, wbf_ref)
    seg = pl.program_id(0) // tiles_per_segment
    acc = _dot(a_ref[...], wbf_ref[...])

    @pl.when(seg == 0)
    def _():
        o_ref[...] = _silu(acc).astype(o_ref.dtype)

    @pl.when(seg == 4)
    def _():
        o_ref[...] = _sigmoid(acc).astype(o_ref.dtype)

    @pl.when((seg != 0) & (seg != 4))
    def _():
        o_ref[...] = acc.astype(o_ref.dtype)


def _mm_resid_bf16w_kernel(a_ref, w_ref, h_ref, g_ref, o_ref):
    o_ref[...] = h_ref[...] + g_ref[...] * _dot(a_ref[...], w_ref[...])


def _side_cast_rows(total_rows, n_steps):
    for rows in range(16, total_rows + 1, 16):
        if total_rows % rows == 0 and total_rows // rows <= n_steps:
            return rows
    raise ValueError((total_rows, n_steps))


class _Rows:
    def __init__(self, batch, seq, ctx_len):
        self.batch, self.seq, self.ctx_len = batch, seq, ctx_len
        self.n_ctx = batch * ctx_len
        self.n_rows = self.n_ctx + batch * seq
        self.groups = 1 + batch
        assert self.n_ctx % ROW_TILE == 0 and seq % ROW_TILE == 0, (batch, seq, ctx_len)
        assert ctx_len % SCAN_ROWS == 0 and seq % SCAN_ROWS == 0 and ctx_len % ATTN_BLOCK == 0

    def group(self, tile, tile_rows=ROW_TILE):
        ctx_tiles = self.n_ctx // tile_rows
        per_batch = self.seq // tile_rows
        return jnp.where(tile < ctx_tiles, 0, 1 + (tile - ctx_tiles) // per_batch)


def _mod_spec(rows, layer, which, width, col_of, tile_rows=ROW_TILE, tile0=0):
    g = rows.groups
    return pl.BlockSpec((None, 1, width),
                        lambda n, i: ((layer * 6 + which) * g + rows.group(tile0 + i, tile_rows), 0, col_of(n)))


def _mm_qkv_rope(a, w, layer, tables, rows):
    m, k = a.shape
    n_total = w.shape[-1]
    tn = QKV_COL_TILE
    ctx_tiles = rows.n_ctx // ROW_TILE
    seq_tiles = rows.seq // ROW_TILE
    tab = lambda n, i: (jnp.where(i < ctx_tiles, i, ctx_tiles + (i - ctx_tiles) % seq_tiles), 0)
    kern = functools.partial(_mm_rope_kernel, n_col_tiles=n_total // tn, n_q_heads=ATTN_HEADS,
                             n_rope_heads=ATTN_HEADS + ATTN_KV_HEADS, q_scale=LOG2_E * HEAD_DIM ** -0.5)
    return pl.pallas_call(
        kern,
        grid=(n_total // tn, m // ROW_TILE),
        in_specs=[pl.BlockSpec((ROW_TILE, k), lambda n, i: (i, 0)),
                  pl.BlockSpec((None, k, tn), lambda n, i: (layer, 0, n)),
                  pl.BlockSpec((ROW_TILE, HEAD_DIM), tab),
                  pl.BlockSpec((ROW_TILE, HEAD_DIM), tab),
                  pl.BlockSpec((ROW_TILE, HEAD_DIM), tab)],
        out_specs=pl.BlockSpec((ROW_TILE, tn), lambda n, i: (i, n)),
        out_shape=jax.ShapeDtypeStruct((m, n_total), BF16),
        scratch_shapes=[pltpu.VMEM((k, tn), BF16)],
        compiler_params=_cparams(2),
        name="mm_qkv_rope",
    )(a, w, *tables)


def _mm_swiglu(a, w13, w2, layer):
    m, k = a.shape
    f = w13.shape[-1] // 2
    d_out = w2.shape[-1]
    tn = SWIGLU_COL_TILE
    nf = f // tn
    n_row_tiles = m // ROW_TILE
    ch = _side_cast_rows(f, nf * n_row_tiles)
    n_chunks = f // ch
    chunk = lambda n, i: jnp.minimum(n * n_row_tiles + i, n_chunks - 1)
    return pl.pallas_call(
        _mm_swiglu_kernel,
        grid=(nf, n_row_tiles),
        in_specs=[pl.BlockSpec((ROW_TILE, k), lambda n, i: (i, 0)),
                  pl.BlockSpec((None, k, tn), lambda n, i: (layer, 0, n)),
                  pl.BlockSpec((None, k, tn), lambda n, i: (layer, 0, n + nf)),
                  pl.BlockSpec((None, ch, d_out), lambda n, i: (layer, chunk(n, i), 0))],
        out_specs=[pl.BlockSpec((ROW_TILE, tn), lambda n, i: (i, n)),
                   pl.BlockSpec((ch, d_out), lambda n, i: (chunk(n, i), 0))],
        out_shape=[jax.ShapeDtypeStruct((m, f), BF16), jax.ShapeDtypeStruct((f, d_out), BF16)],
        scratch_shapes=[pltpu.VMEM((k, tn), BF16), pltpu.VMEM((k, tn), BF16)],
        compiler_params=_cparams(2),
        name="mm_swiglu",
    )(a, w13, w13, w2)


def _mm_hgrn_in(a, w, layer):
    m, k = a.shape
    n_total = w.shape[-1]
    seg = HGRN_HEADS * HGRN_DK
    tn = seg
    assert n_total == 5 * seg
    return pl.pallas_call(
        functools.partial(_mm_hgrn_in_kernel, tiles_per_segment=seg // tn),
        grid=(n_total // tn, m // ROW_TILE),
        in_specs=[pl.BlockSpec((ROW_TILE, k), lambda n, i: (i, 0)),
                  pl.BlockSpec((None, k, tn), lambda n, i: (layer, 0, n))],
        out_specs=pl.BlockSpec((ROW_TILE, tn), lambda n, i: (i, n)),
        out_shape=jax.ShapeDtypeStruct((m, n_total), BF16),
        scratch_shapes=[pltpu.VMEM((k, tn), BF16)],
        compiler_params=_cparams(2),
        name="hgrn_in",
    )(a, w)


def _mm_resid_bf16w(a, w_bf16, h, mods, rows, mod_layer, which, tn, name):
    m, k = a.shape
    n_total = w_bf16.shape[-1]
    return pl.pallas_call(
        _mm_resid_bf16w_kernel,
        grid=(n_total // tn, m // ROW_TILE),
        in_specs=[pl.BlockSpec((ROW_TILE, k), lambda n, i: (i, 0)),
                  pl.BlockSpec((k, tn), lambda n, i: (0, n)),
                  pl.BlockSpec((ROW_TILE, tn), lambda n, i: (i, n)),
                  _mod_spec(rows, mod_layer, which, tn, lambda n: n)],
        out_specs=pl.BlockSpec((ROW_TILE, tn), lambda n, i: (i, n)),
        out_shape=jax.ShapeDtypeStruct((m, n_total), F32),
        compiler_params=_cparams(2),
        name=name,
    )(a, w_bf16, h, mods)


def _cast_bf16_kernel(w_ref, o_ref):
    o_ref[...] = w_ref[...].astype(BF16)


def _cast_bf16(w, layer):
    _, k, n = w.shape
    rows_per_step = 256
    return pl.pallas_call(
        _cast_bf16_kernel,
        grid=(k // rows_per_step,),
        in_specs=[pl.BlockSpec((None, rows_per_step, n), lambda i: (layer, i, 0))],
        out_specs=pl.BlockSpec((rows_per_step, n), lambda i: (i, 0)),
        out_shape=jax.ShapeDtypeStruct((k, n), BF16),
        compiler_params=_cparams(1),
        name="cast_bf16",
    )(w)


def _mixer_out_kernel(a_ref, w_ref, h_ref, g_ref, ng_ref, sh_ref, sc_ref, *rest, route):
    if route:
        r_ref, o_ref, u_ref, route_ref = rest
    else:
        o_ref, u_ref = rest
    h_new = h_ref[...] + g_ref[...] * _dot(a_ref[...], w_ref[...])
    o_ref[...] = h_new
    u = _rms_norm(h_new, ng_ref[...]) * (1.0 + sc_ref[...]) + sh_ref[...]
    u_ref[...] = u.astype(u_ref.dtype)
    if route:
        route_ref[...] = _top2_route(u, r_ref[...])


def _mixer_out(a, w, slot, h, mods, rows, layer, norm_ffn, router=None, moe_layer=0, name="mixer_out"):
    m, k = a.shape
    d = w.shape[-1]
    g = rows.groups
    w_bf16 = _cast_bf16(w, slot)
    mod = lambda which: pl.BlockSpec((None, 1, d), lambda i: ((layer * 6 + which) * g + rows.group(i), 0, 0))
    tile = pl.BlockSpec((ROW_TILE, d), lambda i: (i, 0))
    in_specs = [pl.BlockSpec((ROW_TILE, k), lambda i: (i, 0)),
                pl.BlockSpec((k, d), lambda i: (0, 0)),
                tile, mod(2),
                pl.BlockSpec((None, 1, d), lambda i: (layer, 0, 0)), mod(3), mod(4)]
    args = [a, w_bf16, h, mods, norm_ffn, mods, mods]
    out_specs = [tile, tile]
    if router is None:
        out_shape = [jax.ShapeDtypeStruct((m, d), F32), jax.ShapeDtypeStruct((m, d), BF16)]
    else:
        in_specs.append(pl.BlockSpec((None, d, LANES), lambda i: (moe_layer, 0, 0)))
        args.append(router)
        out_specs.append(pl.BlockSpec((ROW_TILE, LANES), lambda i: (i, 0)))
        out_shape = [jax.ShapeDtypeStruct((m, d), F32), jax.ShapeDtypeStruct((m, d), F32),
                     jax.ShapeDtypeStruct((m, LANES), F32)]
    return pl.pallas_call(
        functools.partial(_mixer_out_kernel, route=router is not None),
        grid=(m // ROW_TILE,),
        in_specs=in_specs,
        out_specs=out_specs,
        out_shape=out_shape,
        compiler_params=_cparams(1),
        name=name,
    )(*args)


def _input_norm(ctx2d, x2d, norm_w, mods, rows):
    d = x2d.shape[1]
    g = rows.groups
    ctx_tiles = rows.n_ctx // ROW_TILE
    mod = lambda which: pl.BlockSpec((None, 1, d), lambda i: (which * g + rows.group(i), 0, 0))
    tile = pl.BlockSpec((ROW_TILE, d), lambda i: (i, 0))
    return pl.pallas_call(
        functools.partial(_input_norm_kernel, ctx_tiles=ctx_tiles),
        grid=(rows.n_rows // ROW_TILE,),
        in_specs=[pl.BlockSpec((ROW_TILE, d), lambda i: (jnp.minimum(i, ctx_tiles - 1), 0)),
                  pl.BlockSpec((ROW_TILE, d), lambda i: (jnp.maximum(i - ctx_tiles, 0), 0)),
                  pl.BlockSpec((None, 1, d), lambda i: (0, 0, 0)), mod(0), mod(1)],
        out_specs=[tile, tile],
        out_shape=[jax.ShapeDtypeStruct((rows.n_rows, d), F32), jax.ShapeDtypeStruct((rows.n_rows, d), BF16)],
        compiler_params=_cparams(1),
        name="input_norm",
    )(ctx2d, x2d, norm_w, mods, mods)


def _norm_mod(h, norm_w, mods, rows, layer, which_shift):
    m, d = h.shape
    return pl.pallas_call(
        _norm_mod_kernel,
        grid=(1, m // ROW_TILE),
        in_specs=[pl.BlockSpec((ROW_TILE, d), lambda n, i: (i, 0)),
                  pl.BlockSpec((None, 1, d), lambda n, i: (layer, 0, 0)),
                  _mod_spec(rows, layer, which_shift, d, lambda n: 0),
                  _mod_spec(rows, layer, which_shift + 1, d, lambda n: 0)],
        out_specs=pl.BlockSpec((ROW_TILE, d), lambda n, i: (i, 0)),
        out_shape=jax.ShapeDtypeStruct((m, d), BF16),
        compiler_params=_cparams(2),
        name="norm_mod",
    )(h, norm_w, mods, mods)


def _final_norm(h, norm_w, rows):
    d = h.shape[1]
    first = rows.n_ctx // ROW_TILE
    n_lat = rows.batch * rows.seq
    return pl.pallas_call(
        _final_norm_kernel,
        grid=(n_lat // ROW_TILE,),
        in_specs=[pl.BlockSpec((ROW_TILE, d), lambda i: (first + i, 0)),
                  pl.BlockSpec((1, d), lambda i: (0, 0))],
        out_specs=pl.BlockSpec((ROW_TILE, d), lambda i: (i, 0)),
        out_shape=jax.ShapeDtypeStruct((n_lat, d), F32),
        compiler_params=_cparams(1),
        name="final_norm",
    )(h, norm_w.reshape(1, d))


def _rope_tables(rows):
    s = rows.seq
    pos = np.arange(s)
    row = (pos // GRID_W).astype(np.float64)
    col = (pos % GRID_W).astype(np.float64)
    sec = HEAD_DIM // 2
    inv = ROPE_THETA ** (-np.arange(0, sec, 2, dtype=np.float64) / sec)
    inv = inv.astype(np.float32).astype(np.float64)
    ang = np.concatenate([row[:, None] * inv, row[:, None] * inv, col[:, None] * inv, col[:, None] * inv], axis=1)
    ang = ang.astype(np.float32).astype(np.float64)
    cos, sin = np.cos(ang), np.sin(ang)
    first_half = (np.arange(HEAD_DIM) % sec) < (sec // 2)
    sa = np.where(first_half[None, :], -sin, 0.0)
    sb = np.where(first_half[None, :], 0.0, sin)
    ident = np.zeros((rows.n_ctx, HEAD_DIM))
    mk = lambda ctx_rows, lat: jnp.asarray(np.concatenate([ctx_rows, lat], axis=0), dtype=F32)
    return mk(ident + 1.0, cos), mk(ident, sa), mk(ident, sb)


def _attn_block(q_ref, o_ref, sink, s_keys, v_keys, bias):
    q = jnp.concatenate([q_ref[:, g * HEAD_DIM:(g + 1) * HEAD_DIM] for g in range(ATTN_GROUP)], axis=0)
    s_c = _dot_nt(q, s_keys[0])
    m = jnp.maximum(jnp.max(s_c, axis=-1, keepdims=True), sink)
    if bias is not None:
        s_b = _dot_nt(q, s_keys[1]) + jnp.concatenate([bias] * ATTN_GROUP, axis=0)
        m = jnp.maximum(m, jnp.max(s_b, axis=-1, keepdims=True))
    p_c = jnp.exp2(s_c - m)
    den = jnp.sum(p_c, axis=-1, keepdims=True) + jnp.exp2(sink - m)
    o = _dot(p_c.astype(BF16), v_keys[0])
    if bias is not None:
        p_b = jnp.exp2(s_b - m)
        den = den + jnp.sum(p_b, axis=-1, keepdims=True)
        o = o + _dot(p_b.astype(BF16), v_keys[1])
    o = o * (1.0 / den)
    for g in range(ATTN_GROUP):
        o_ref[:, g * HEAD_DIM:(g + 1) * HEAD_DIM] = o[g * ATTN_BLOCK:(g + 1) * ATTN_BLOCK].astype(o_ref.dtype)


def _attn_kernel(sink_ref, q_ref, kc_ref, vc_ref, kp_ref, ko_ref, kn_ref, vp_ref, vo_ref, vn_ref, bias_ref, o_ref, *,
                 slot, n_ctx_blocks):
    hkv = pl.program_id(1)
    step = pl.program_id(2)
    rows_q = ATTN_GROUP * ATTN_BLOCK
    grp = lax.broadcasted_iota(jnp.int32, (rows_q, 1), 0) // ATTN_BLOCK
    sink = jnp.zeros((rows_q, 1), F32)
    for g in range(ATTN_GROUP):
        sink = jnp.where(grp == g, sink_ref[slot, hkv * ATTN_GROUP + g] * LOG2_E, sink)

    @pl.when(step < n_ctx_blocks)
    def _():
        _attn_block(q_ref, o_ref, sink, (kc_ref[...],), (vc_ref[...],), None)

    @pl.when(step >= n_ctx_blocks)
    def _():
        kb = jnp.concatenate([kp_ref[...], ko_ref[...], kn_ref[...]], axis=0)
        vb = jnp.concatenate([vp_ref[...], vo_ref[...], vn_ref[...]], axis=0)
        _attn_block(q_ref, o_ref, sink, (kc_ref[...], kb), (vc_ref[...], vb), bias_ref[...])


def _band_bias():
    i = np.arange(ATTN_BLOCK)[:, None]
    j = np.arange(3 * ATTN_BLOCK)[None, :]
    window = (j >= i) & (j <= i + 2 * ATTN_BLOCK)
    first = window & (j >= ATTN_BLOCK)
    last = window & (j < 2 * ATTN_BLOCK)
    return jnp.asarray(np.where(np.stack([first, window, last]), 0.0, NEG_INF), dtype=F32)


def _attention(p, sink, slot, rows):
    r = p.shape[0]
    dq = ATTN_HEADS * HEAD_DIM
    gw = ATTN_GROUP * HEAD_DIM
    kcol = dq // HEAD_DIM
    vcol = kcol + ATTN_KV_HEADS
    L, S, B = rows.ctx_len, rows.seq, rows.batch
    nb = S // ATTN_BLOCK
    nbc = L // ATTN_BLOCK
    lat0 = rows.n_ctx // ATTN_BLOCK
    assert nb >= 2

    def q_block(b, s):
        return jnp.where(s < nbc, b * nbc + s, lat0 + b * nb + (s - nbc))

    def band(col0, shift):
        return pl.BlockSpec((ATTN_BLOCK, HEAD_DIM),
                            lambda b, h, s: (lat0 + b * nb + jnp.clip(s - nbc + shift, 0, nb - 1), col0 + h))

    return pl.pallas_call(
        functools.partial(_attn_kernel, slot=slot, n_ctx_blocks=nbc),
        grid=(B, ATTN_KV_HEADS, nbc + nb),
        in_specs=[pl.BlockSpec(memory_space=pltpu.SMEM),
                  pl.BlockSpec((ATTN_BLOCK, gw), lambda b, h, s: (q_block(b, s), h)),
                  pl.BlockSpec((L, HEAD_DIM), lambda b, h, s: (b, kcol + h)),
                  pl.BlockSpec((L, HEAD_DIM), lambda b, h, s: (b, vcol + h)),
                  band(kcol, -1), band(kcol, 0), band(kcol, 1), band(vcol, -1), band(vcol, 0), band(vcol, 1),
                  pl.BlockSpec((None, ATTN_BLOCK, 3 * ATTN_BLOCK),
                               lambda b, h, s: (jnp.where(s <= nbc, 0, jnp.where(s == nbc + nb - 1, 2, 1)), 0, 0))],
        out_specs=pl.BlockSpec((ATTN_BLOCK, gw), lambda b, h, s: (q_block(b, s), h)),
        out_shape=jax.ShapeDtypeStruct((r, dq), BF16),
        compiler_params=_cparams(3),
        name="attention",
    )(sink, p, p, p, p, p, p, p, p, p, _band_bias())


def _dft_cs(n, scale):
    k = np.arange(n)
    ang = 2.0 * np.pi * ((k[:, None] * k[None, :]) % n) / n
    return np.cos(ang) * scale, np.sin(ang) * scale


def _chan_dft_kernel(u_ref, m_ref, o_ref, *, gd):
    for g in range(u_ref.shape[1] // gd):
        v = _dot(u_ref[:, g * gd:(g + 1) * gd], m_ref[...])
        o_ref[0, :, g * gd:(g + 1) * gd] = v[:, :gd].astype(o_ref.dtype)
        o_ref[1, :, g * gd:(g + 1) * gd] = v[:, gd:].astype(o_ref.dtype)


def _chan_dft(u, row0, n_rows, mat):
    d = u.shape[1]
    gd = d // FNET_GROUPS
    t0 = row0 // ROW_TILE
    return pl.pallas_call(
        functools.partial(_chan_dft_kernel, gd=gd),
        grid=(n_rows // ROW_TILE,),
        in_specs=[pl.BlockSpec((ROW_TILE, d), lambda i: (t0 + i, 0)),
                  pl.BlockSpec((gd, 2 * gd), lambda i: (0, 0))],
        out_specs=pl.BlockSpec((2, ROW_TILE, d), lambda i: (0, i, 0)),
        out_shape=jax.ShapeDtypeStruct((2, n_rows, d), BF16),
        compiler_params=_cparams(1),
        name="fnet_chan_dft",
    )(u, mat)


def _seq_dft_a_kernel(x_ref, chan_ref, m_ref, tc_ref, ts_ref, o_ref, *, gd):
    n1 = x_ref.shape[0]
    d = o_ref.shape[-1]
    for j in range(o_ref.shape[1]):
        parts = [_dot(x_ref[:, j * d + g * gd:j * d + (g + 1) * gd], chan_ref[...]) for g in range(d // gd)]
        vr = jnp.concatenate([p[:, :gd] for p in parts], axis=1).astype(BF16)
        vi = jnp.concatenate([p[:, gd:] for p in parts], axis=1).astype(BF16)
        z = _dot(m_ref[...], jnp.concatenate([vr, vi], axis=0))
        zr, zi = z[:n1], z[n1:]
        tc, ts = tc_ref[j], ts_ref[j]
        for c in range(d // LANES):
            sl = slice(c * LANES, (c + 1) * LANES)
            o_ref[0, j, :, sl] = (zr[:, sl] * tc + zi[:, sl] * ts).astype(o_ref.dtype)
            o_ref[1, j, :, sl] = (zi[:, sl] * tc - zr[:, sl] * ts).astype(o_ref.dtype)


def _seq_dft_c_kernel(x_ref, m_ref, o_ref):
    x = jnp.concatenate([x_ref[0], x_ref[1]], axis=0)
    o_ref[...] = _dot(m_ref[...], x).astype(o_ref.dtype)


def _seq_dft_c_blocked_kernel(x_ref, m_ref, o_ref):
    two, n2, kb, d = x_ref.shape
    x = x_ref[...].reshape(two * n2 * kb, d)
    o_ref[...] = _dot(m_ref[...], x).reshape(n2, kb, d).astype(o_ref.dtype)


def _fourier_tokens(u, rows):
    d = u.shape[1]
    gd = d // FNET_GROUPS
    B, S, L = rows.batch, rows.seq, rows.ctx_len
    n2 = GRID_W
    n1 = S // n2
    cc, sc = _dft_cs(gd, gd ** -0.5)
    chan = jnp.asarray(np.concatenate([cc, -sc], axis=1), dtype=BF16)

    x_lat = u[rows.n_ctx:].reshape(B, n1, n2 * d)
    c1, s1 = _dft_cs(n1, n1 ** -0.5)
    m1 = jnp.asarray(np.block([[c1, s1], [-s1, c1]]), dtype=BF16)
    ang = 2.0 * np.pi * (np.arange(n2)[:, None] * np.arange(n1)[None, :]) / S
    tw_c = jnp.asarray(np.broadcast_to(np.cos(ang)[:, :, None], (n2, n1, LANES)), dtype=F32)
    tw_s = jnp.asarray(np.broadcast_to(np.sin(ang)[:, :, None], (n2, n1, LANES)), dtype=F32)
    tb = DFT_T2_BLOCK
    z = pl.pallas_call(
        functools.partial(_seq_dft_a_kernel, gd=gd),
        grid=(B, n2 // tb),
        in_specs=[pl.BlockSpec((None, n1, tb * d), lambda b, t: (b, 0, t)),
                  pl.BlockSpec((gd, 2 * gd), lambda b, t: (0, 0)),
                  pl.BlockSpec((2 * n1, 2 * n1), lambda b, t: (0, 0)),
                  pl.BlockSpec((tb, n1, LANES), lambda b, t: (t, 0, 0)),
                  pl.BlockSpec((tb, n1, LANES), lambda b, t: (t, 0, 0))],
        out_specs=pl.BlockSpec((2, None, tb, n1, d), lambda b, t: (0, b, t, 0, 0)),
        out_shape=jax.ShapeDtypeStruct((2, B, n2, n1, d), BF16),
        compiler_params=_cparams(2),
        name="fnet_seq_dft_a",
    )(x_lat, chan, m1, tw_c, tw_s)
    c2, s2 = _dft_cs(n2, n2 ** -0.5)
    kb = DFT_POS_BLOCK
    m2 = jnp.asarray(np.kron(np.concatenate([c2, s2], axis=1), np.eye(kb)), dtype=BF16)
    y_lat = pl.pallas_call(
        _seq_dft_c_blocked_kernel,
        grid=(B, n1 // kb),
        in_specs=[pl.BlockSpec((2, None, n2, kb, d), lambda b, j: (0, b, 0, j, 0)),
                  pl.BlockSpec((n2 * kb, 2 * n2 * kb), lambda b, j: (0, 0))],
        out_specs=pl.BlockSpec((None, n2, kb, d), lambda b, j: (b, 0, j, 0)),
        out_shape=jax.ShapeDtypeStruct((B, n2, n1, d), BF16),
        compiler_params=_cparams(2),
        name="fnet_seq_dft_c",
    )(z, m2)

    vc = _chan_dft(u, 0, B * L, chan).reshape(2, B, L, d)
    cl, sl = _dft_cs(L, L ** -0.5)
    ml = jnp.asarray(np.concatenate([cl, sl], axis=1), dtype=BF16)
    y_ctx = pl.pallas_call(
        _seq_dft_c_kernel,
        grid=(B, 1),
        in_specs=[pl.BlockSpec((2, None, L, d), lambda b, j: (0, b, 0, 0)),
                  pl.BlockSpec((L, 2 * L), lambda b, j: (0, 0))],
        out_specs=pl.BlockSpec((None, L, d), lambda b, j: (b, 0, 0)),
        out_shape=jax.ShapeDtypeStruct((B, L, d), BF16),
        compiler_params=_cparams(2),
        name="fnet_ctx_dft",
    )(vc, ml)
    return jnp.concatenate([y_ctx.reshape(B * L, d), y_lat.reshape(B * S, d)], axis=0)


def _hgrn_scan_kernel(q_ref, f_ref, v_ref, lb_ref, tri_ref, o_ref, st_ref, *, reverse):
    @pl.when(pl.program_id(1) == 0)
    def _():
        st_ref[...] = jnp.zeros(st_ref.shape, st_ref.dtype)

    c = HGRN_CHUNK
    n_sub = q_ref.shape[0] // c
    lb = lb_ref[...]
    tri = tri_ref[...]
    ti = lax.broadcasted_iota(jnp.int32, (c, c), 0)
    si = lax.broadcasted_iota(jnp.int32, (c, c), 1)
    keep = (si >= ti) if reverse else (si <= ti)
    order = range(n_sub - 1, -1, -1) if reverse else range(n_sub)
    for sub in order:
        rs = slice(sub * c, (sub + 1) * c)
        q = q_ref[rs, :].astype(F32)
        f = lb + (1.0 - lb) * _sigmoid(f_ref[rs, :].astype(F32))
        k = 1.0 - f
        bsum = jnp.dot(tri, jnp.log(f), preferred_element_type=F32, precision=lax.Precision.HIGHEST)
        b_end = bsum[0:1, :] if reverse else bsum[c - 1:c, :]
        decay = jnp.exp(b_end)
        q_in = (q * jnp.exp(bsum)).astype(BF16)
        k_inf = k * jnp.exp(-bsum)
        k_in = k_inf.astype(BF16)
        k_out = (k_inf * decay).astype(BF16)
        v = v_ref[rs, :]
        vt = v.astype(F32)
        for h in range(HGRN_HEADS):
            ks = slice(h * HGRN_DK, (h + 1) * HGRN_DK)
            vs = slice(h * HGRN_DV, (h + 1) * HGRN_DV)
            a = jnp.where(keep, _dot_nt(q_in[:, ks], k_in[:, ks]), 0.0)
            st = st_ref[h]
            o = _dot(a.astype(BF16), v[:, vs]) + _dot_nt(q_in[:, ks], st.astype(BF16))
            o_ref[rs, vs] = o
            st_ref[h] = st * decay[:, ks] + _dot(vt[:, vs].T.astype(BF16), k_out[:, ks])


def _hgrn_scan(pm, lb, rows, reverse):
    r = pm.shape[0]
    hk = HGRN_HEADS * HGRN_DK
    B, S, L = rows.batch, rows.seq, rows.ctx_len
    cs, ls = L // SCAN_ROWS, S // SCAN_ROWS
    direction = 1 if reverse else 0

    def row_block(b, s):
        if reverse:
            ctx = b * cs + (cs - 1 - s)
            lat = B * cs + b * ls + (ls - 1 - (s - cs))
        else:
            ctx = b * cs + s
            lat = B * cs + b * ls + (s - cs)
        return jnp.where(s < cs, ctx, lat)

    c = HGRN_CHUNK
    tri_np = np.triu(np.ones((c, c))) if reverse else np.tril(np.ones((c, c)))
    tri = jnp.asarray(tri_np, dtype=F32)
    col = lambda j: pl.BlockSpec((SCAN_ROWS, hk), lambda b, s: (row_block(b, s), j))
    return pl.pallas_call(
        functools.partial(_hgrn_scan_kernel, reverse=reverse),
        grid=(B, cs + ls),
        in_specs=[col(0), col(1 + direction), col(3),
                  pl.BlockSpec((None, 1, hk), lambda b, s: (direction, 0, 0)),
                  pl.BlockSpec((c, c), lambda b, s: (0, 0))],
        out_specs=pl.BlockSpec((SCAN_ROWS, HGRN_HEADS * HGRN_DV), lambda b, s: (row_block(b, s), 0)),
        out_shape=jax.ShapeDtypeStruct((r, HGRN_HEADS * HGRN_DV), F32),
        scratch_shapes=[pltpu.VMEM((HGRN_HEADS, HGRN_DV, HGRN_DK), F32)],
        compiler_params=_cparams(2),
        name="hgrn_scan_bwd" if reverse else "hgrn_scan_fwd",
    )(pm, pm, pm, lb, tri)


def _hgrn_readout_kernel(of_ref, ob_ref, g_ref, ng_ref, o_ref):
    for h in range(HGRN_HEADS):
        vs = slice(h * HGRN_DV, (h + 1) * HGRN_DV)
        o = of_ref[:, vs] + ob_ref[:, vs]
        o = o * lax.rsqrt(jnp.mean(o * o, axis=-1, keepdims=True) + RMS_EPS) * ng_ref[:, vs]
        o_ref[:, vs] = (o * g_ref[:, vs].astype(F32)).astype(o_ref.dtype)


def _hgrn_readout(o_f, o_b, pm, norm_g, slot):
    r, d = o_f.shape
    gcol = pm.shape[1] // d - 1
    return pl.pallas_call(
        _hgrn_readout_kernel,
        grid=(r // ROW_TILE,),
        in_specs=[pl.BlockSpec((ROW_TILE, d), lambda i: (i, 0)),
                  pl.BlockSpec((ROW_TILE, d), lambda i: (i, 0)),
                  pl.BlockSpec((ROW_TILE, d), lambda i: (i, gcol)),
                  pl.BlockSpec((None, 1, d), lambda i: (slot, 0, 0))],
        out_specs=pl.BlockSpec((ROW_TILE, d), lambda i: (i, 0)),
        out_shape=jax.ShapeDtypeStruct((r, d), BF16),
        compiler_params=_cparams(1),
        name="hgrn_readout",
    )(o_f, o_b, pm, norm_g)


def _moe_plan(route, n_tiles):
    e_flat = jnp.concatenate([route[:, 0], route[:, 1]]).astype(jnp.int32)
    onehot = (e_flat[:, None] == jnp.arange(N_EXPERTS, dtype=jnp.int32)[None, :]).astype(jnp.int32)
    csum = jnp.cumsum(onehot, axis=0)
    counts = csum[-1]
    rank = jnp.sum(csum * onehot, axis=1) - 1
    padded = ((counts + MOE_ROW_TILE - 1) // MOE_ROW_TILE) * MOE_ROW_TILE
    ends = jnp.cumsum(padded)
    starts = ends - padded
    dest = jnp.sum(starts[None, :] * onehot, axis=1) + rank
    n_used = (ends[-1] // MOE_ROW_TILE).astype(jnp.int32)
    tile = jnp.arange(n_tiles, dtype=jnp.int32)
    te = jnp.sum((ends[None, :] <= (tile * MOE_ROW_TILE)[:, None]).astype(jnp.int32), axis=1)
    te = jnp.minimum(te, N_EXPERTS - 1)
    te_last = jnp.sum(jnp.where(tile == n_used - 1, te, 0))
    te = jnp.where(tile < n_used, te, te_last).astype(jnp.int32)
    pad = jnp.stack([starts + counts, ends]).astype(jnp.int32)
    return dest, te, n_used.reshape(1), pad.reshape(-1)


def _moe_gather_kernel(dest_ref, nu_ref, pad_ref, src_ref, o_ref, tok_ref, buf_ref, sem, *, n_tokens, row0):
    i = pl.program_id(0)
    n_used = nu_ref[0]

    def row_copy(slot, r, t):
        return pltpu.make_async_copy(src_ref.at[pl.ds(t, 1), :], buf_ref.at[slot, pl.ds(r, 1), :], sem.at[slot])

    def issue_tile(tile):
        slot = tile % 2
        base = tile * MOE_ROW_TILE

        def issue(r, carry):
            row_copy(slot, r, tok_ref[base + r]).start()
            return carry

        lax.fori_loop(0, MOE_ROW_TILE, issue, 0, unroll=8)

    @pl.when(i == 0)
    def _():
        def fill(p, carry):
            tok_ref[p] = row0
            return carry

        def place(t, carry):
            tok_ref[dest_ref[t]] = row0 + t
            tok_ref[dest_ref[n_tokens + t]] = row0 + t
            return carry

        for e in range(N_EXPERTS):
            lax.fori_loop(pad_ref[e], pad_ref[N_EXPERTS + e], fill, 0)
        lax.fori_loop(0, n_tokens, place, 0, unroll=8)
        issue_tile(i)

    @pl.when(i + 1 < n_used)
    def _():
        issue_tile(i + 1)

    @pl.when(i < n_used)
    def _():
        slot = i % 2

        def drain(r, carry):
            row_copy(slot, r, 0).wait()
            return carry

        lax.fori_loop(0, MOE_ROW_TILE, drain, 0, unroll=8)
        o_ref[...] = buf_ref[slot].astype(o_ref.dtype)

    @pl.when(i >= n_used)
    def _():
        o_ref[...] = jnp.zeros(o_ref.shape, o_ref.dtype)


def _moe_gather(dest, n_used, pad, u, n_tiles, row0):
    d = u.shape[1]
    p_rows = n_tiles * MOE_ROW_TILE
    return pl.pallas_call(
        functools.partial(_moe_gather_kernel, n_tokens=dest.shape[0] // 2, row0=row0),
        grid_spec=pltpu.PrefetchScalarGridSpec(
            num_scalar_prefetch=3,
            grid=(n_tiles,),
            in_specs=[pl.BlockSpec(memory_space=pl.ANY)],
            out_specs=pl.BlockSpec((MOE_ROW_TILE, d), lambda i, dst, nu, pd: (i, 0)),
            scratch_shapes=[pltpu.SMEM((p_rows,), jnp.int32),
                            pltpu.VMEM((2, MOE_ROW_TILE, d), F32), pltpu.SemaphoreType.DMA((2,))]),
        out_shape=jax.ShapeDtypeStruct((p_rows, d), BF16),
        compiler_params=_cparams(1),
        name="moe_gather",
    )(dest, n_used, pad, u)


def _expert_changed(te_ref):
    i = pl.program_id(1)
    return (i == 0) | (te_ref[i] != te_ref[jnp.maximum(i - 1, 0)])


def _gmm_swiglu_kernel(te_ref, nu_ref, x_ref, wa_ref, wb_ref, w2_ref, o_ref, w2bf_ref, wbfa_ref, wbfb_ref):
    i = pl.program_id(1)

    @pl.when(_expert_changed(te_ref))
    def _():
        wbfa_ref[...] = wa_ref[...].astype(BF16)
        wbfb_ref[...] = wb_ref[...].astype(BF16)

    @pl.when(i < nu_ref[0])
    def _():
        x = x_ref[...]
        o_ref[...] = (_silu(_dot(x, wbfa_ref[...])) * _dot(x, wbfb_ref[...])).astype(o_ref.dtype)

    @pl.when(i >= nu_ref[0])
    def _():
        o_ref[...] = jnp.zeros(o_ref.shape, o_ref.dtype)

    w2bf_ref[...] = w2_ref[...].astype(BF16)


def _gmm_down_kernel(te_ref, nu_ref, x_ref, w_ref, o_ref):
    i = pl.program_id(1)

    @pl.when(i < nu_ref[0])
    def _():
        o_ref[...] = _dot(x_ref[...], w_ref[...])

    @pl.when(i >= nu_ref[0])
    def _():
        o_ref[...] = jnp.zeros(o_ref.shape, o_ref.dtype)


def _moe_experts(xg, te, n_used, w13, w2, layer, n_tiles):
    p_rows, d = xg.shape
    n_layers, n_exp, f, d_out = w2.shape
    tn = COL_TILE
    nf = f // tn
    used = lambda i, nu: jnp.minimum(i, nu[0] - 1)
    ch = _side_cast_rows(n_exp * f, nf * n_tiles)
    n_chunks = n_exp * f // ch
    chunk = lambda n, i: jnp.minimum(n * n_tiles + i, n_chunks - 1)
    hidden, w2_bf16 = pl.pallas_call(
        _gmm_swiglu_kernel,
        grid_spec=pltpu.PrefetchScalarGridSpec(
            num_scalar_prefetch=2,
            grid=(nf, n_tiles),
            in_specs=[pl.BlockSpec((MOE_ROW_TILE, d), lambda n, i, te, nu: (used(i, nu), 0)),
                      pl.BlockSpec((None, None, d, tn), lambda n, i, te, nu: (layer, te[i], 0, n)),
                      pl.BlockSpec((None, None, d, tn), lambda n, i, te, nu: (layer, te[i], 0, n + nf)),
                      pl.BlockSpec((None, ch, d_out), lambda n, i, te, nu: (layer, chunk(n, i), 0))],
            out_specs=[pl.BlockSpec((MOE_ROW_TILE, tn), lambda n, i, te, nu: (i, n)),
                       pl.BlockSpec((ch, d_out), lambda n, i, te, nu: (chunk(n, i), 0))],
            scratch_shapes=[pltpu.VMEM((d, tn), BF16), pltpu.VMEM((d, tn), BF16)]),
        out_shape=[jax.ShapeDtypeStruct((p_rows, f), BF16), jax.ShapeDtypeStruct((n_exp * f, d_out), BF16)],
        compiler_params=_cparams(2),
        name="moe_w13",
    )(te, n_used, xg, w13, w13, w2.reshape(n_layers, n_exp * f, d_out))
    tn2 = MOE_W2_COL_TILE
    return pl.pallas_call(
        _gmm_down_kernel,
        grid_spec=pltpu.PrefetchScalarGridSpec(
            num_scalar_prefetch=2,
            grid=(d_out // tn2, n_tiles),
            in_specs=[pl.BlockSpec((MOE_ROW_TILE, f), lambda n, i, te, nu: (used(i, nu), 0)),
                      pl.BlockSpec((None, f, tn2), lambda n, i, te, nu: (te[i], 0, n))],
            out_specs=pl.BlockSpec((MOE_ROW_TILE, tn2), lambda n, i, te, nu: (i, n)),
            scratch_shapes=[]),
        out_shape=jax.ShapeDtypeStruct((p_rows, d_out), F32),
        compiler_params=_cparams(2),
        name="moe_w2",
    )(te, n_used, hidden, w2_bf16.reshape(n_exp, f, d_out))


def _moe_combine_kernel(p1_ref, p2_ref, y_ref, h_ref, g_ref, route_ref, *rest, mode):
    if mode == "final":
        ng_ref, o_ref, buf_ref, sem = rest
    else:
        ng_ref, sh_ref, sc_ref, o_ref, u_ref, buf_ref, sem = rest
    i = pl.program_id(0)

    def row_copy(slot, k, r, p):
        return pltpu.make_async_copy(y_ref.at[pl.ds(p, 1), :], buf_ref.at[slot, k, pl.ds(r, 1), :], sem.at[slot])

    def issue_tile(tile):
        slot = tile % 2
        base = tile * COMBINE_ROW_TILE

        def issue(r, carry):
            row_copy(slot, 0, r, p1_ref[base + r]).start()
            row_copy(slot, 1, r, p2_ref[base + r]).start()
            return carry

        lax.fori_loop(0, COMBINE_ROW_TILE, issue, 0, unroll=8)

    @pl.when(i == 0)
    def _():
        issue_tile(i)

    @pl.when(i + 1 < pl.num_programs(0))
    def _():
        issue_tile(i + 1)

    slot = i % 2

    def drain(r, carry):
        row_copy(slot, 0, r, 0).wait()
        row_copy(slot, 1, r, 0).wait()
        return carry

    lax.fori_loop(0, COMBINE_ROW_TILE, drain, 0, unroll=8)
    w1 = route_ref[:, 2:3]
    w2 = route_ref[:, 3:4]
    h_new = h_ref[...] + g_ref[...] * (w1 * buf_ref[slot, 0] + w2 * buf_ref[slot, 1])
    if mode == "final":
        o_ref[...] = _rms_norm(h_new, ng_ref[...])
    else:
        o_ref[...] = h_new
        u_ref[...] = (_rms_norm(h_new, ng_ref[...]) * (1.0 + sc_ref[...]) + sh_ref[...]).astype(u_ref.dtype)


def _moe_combine(pos1, pos2, y, h, mods, route, rows, mod_layer, row0, post):
    d = h.shape[1]
    r = h.shape[0] - row0
    g = rows.groups
    tr = COMBINE_ROW_TILE
    t0 = row0 // tr
    mod = lambda layer, which: pl.BlockSpec(
        (None, 1, d), lambda i, p1, p2: ((layer * 6 + which) * g + rows.group(t0 + i, tr), 0, 0))
    tile = pl.BlockSpec((tr, d), lambda i, p1, p2: (i, 0))
    in_specs = [pl.BlockSpec(memory_space=pl.ANY),
                pl.BlockSpec((tr, d), lambda i, p1, p2: (t0 + i, 0)),
                mod(mod_layer, 5),
                pl.BlockSpec((tr, LANES), lambda i, p1, p2: (t0 + i, 0))]
    if post[0] == "final":
        in_specs.append(pl.BlockSpec((1, d), lambda i, p1, p2: (0, 0)))
        extra = (post[1],)
        out_specs, out_shape = tile, jax.ShapeDtypeStruct((r, d), F32)
    else:
        _, norm_w, nxt = post
        in_specs += [pl.BlockSpec((None, 1, d), lambda i, p1, p2: (nxt, 0, 0)), mod(nxt, 0), mod(nxt, 1)]
        extra = (norm_w, mods, mods)
        out_specs = [tile, tile]
        out_shape = [jax.ShapeDtypeStruct((r, d), F32), jax.ShapeDtypeStruct((r, d), BF16)]
    return pl.pallas_call(
        functools.partial(_moe_combine_kernel, mode=post[0]),
        grid_spec=pltpu.PrefetchScalarGridSpec(
            num_scalar_prefetch=2,
            grid=(r // tr,),
            in_specs=in_specs,
            out_specs=out_specs,
            scratch_shapes=[pltpu.VMEM((2, 2, tr, d), F32), pltpu.SemaphoreType.DMA((2,))]),
        out_shape=out_shape,
        compiler_params=_cparams(1),
        name="moe_combine",
    )(pos1, pos2, y, h, mods, route, *extra)


def _moe_ffn(h, u, route, mods, rows, layer, w13, w2, moe_layer, row0, post):
    r = h.shape[0] - row0
    n_tiles = -(-(2 * r + N_EXPERTS * (MOE_ROW_TILE - 1)) // MOE_ROW_TILE)
    dest, te, n_used, pad = _moe_plan(route[row0:], n_tiles)
    xg = _moe_gather(dest, n_used, pad, u, n_tiles, row0)
    y = _moe_experts(xg, te, n_used, w13, w2, moe_layer, n_tiles)
    return _moe_combine(dest[:r], dest[r:], y, h, mods, route, rows, layer, row0, post)


def _hgrn_lower_bounds(lb_logits, layer):
    gamma = jax.nn.softmax(lb_logits.astype(F32), axis=0)
    lb = jnp.cumsum(gamma, axis=0) - gamma[0]
    return lb[layer]


def kernel(x, c, ctx, c_ctx, ada_w, ada_b, norm_mix, norm_ffn, norm_final, attn_wqkv, attn_wo, attn_sink, fnet_wo,
           hgrn_win, hgrn_lb, hgrn_norm, hgrn_wo, ffn_w13, ffn_w2, moe_router, moe_w13, moe_w2):
    B, S, D = x.shape
    L = ctx.shape[1]
    depth = ada_w.shape[0]
    rows = _Rows(B, S, L)
    G = rows.groups

    cond = jnp.concatenate([c_ctx[None, :], c], axis=0)
    cond = jnp.pad(_silu(cond), ((0, 16 - G), (0, 0))).astype(BF16)
    mods = _ada_mods(cond, ada_w, ada_b)
    mods = mods[:, :G, :].reshape(depth, G, 6, D).transpose(0, 2, 1, 3).reshape(depth * 6 * G, 1, D)

    norm_mix3 = norm_mix.reshape(depth, 1, D)
    h, u = _input_norm(ctx.reshape(B * L, D), x.reshape(B * S, D), norm_mix3, mods, rows)
    norm_ffn3 = norm_ffn.reshape(depth, 1, D)
    rope_tables = _rope_tables(rows)
    router_pad = jnp.pad(moe_router, ((0, 0), (0, 0), (0, LANES - moe_router.shape[-1]))).astype(BF16)

    for i in range(depth):
        kind, slot = i % N_MIXERS, i // N_MIXERS
        j = i // 2
        dense = i % 2 == 0
        if u is None:
            u = _norm_mod(h, norm_mix3, mods, rows, i, 0)
        if kind == 0:
            p = _mm_qkv_rope(u, attn_wqkv, slot, rope_tables, rows)
            o, w_out, name = _attention(p, attn_sink, slot, rows), attn_wo, "attn_out"
        elif kind == 1:
            o, w_out, name = _fourier_tokens(u, rows), fnet_wo, "fnet_out"
        else:
            lb = _hgrn_lower_bounds(hgrn_lb, i).reshape(2, 1, HGRN_HEADS * HGRN_DK)
            pm = _mm_hgrn_in(u, hgrn_win, slot)
            o_f = _hgrn_scan(pm, lb, rows, reverse=False)
            o_b = _hgrn_scan(pm, lb, rows, reverse=True)
            o, w_out, name = _hgrn_readout(o_f, o_b, pm, hgrn_norm.reshape(-1, 1, D), slot), hgrn_wo, "hgrn_out"
        u = None

        if dense:
            h, v = _mixer_out(o, w_out, slot, h, mods, rows, i, norm_ffn3, name=name)
            g, w2_bf16 = _mm_swiglu(v, ffn_w13, ffn_w2, j)
            h = _mm_resid_bf16w(g, w2_bf16, h, mods, rows, i, 5, FFN_W2_COL_TILE, name="ffn_out")
            continue
        h, v, route = _mixer_out(o, w_out, slot, h, mods, rows, i, norm_ffn3, router_pad, j, name=name)
        if i == depth - 1:
            out = _moe_ffn(h, v, route, mods, rows, i, moe_w13, moe_w2, j, rows.n_ctx,
                           ("final", norm_final.reshape(1, D)))
            return out.reshape(B, S, D)
        h, u = _moe_ffn(h, v, route, mods, rows, i, moe_w13, moe_w2, j, 0, ("next", norm_mix3, i + 1))

    return _final_norm(h, norm_final, rows).reshape(B, S, D)
```
